```python
import jax, jax.numpy as jnp
from jax import lax
import numpy as np

D_MODEL = 1024
BATCH = 2
SEQ = 16384
DEPTH = 4
DEC_BATCH = 4
DEC_SEQ = 4096
PAST_LEN = 128

N_MIXERS = 3
CONV_WIDTH = 3
FNET_GROUPS = 8
FNET_GROUP_DIM = D_MODEL // FNET_GROUPS
GMLP_FFN = 6 * D_MODEL
GMLP_HALF = GMLP_FFN // 2
GMLP_HEADS = 8
GMLP_HEAD_DIM = GMLP_HALF // GMLP_HEADS
CHUNK = 128
D_FF = 7 * D_MODEL // 2
N_EXPERTS = 8
TOP_K = 2
EPS = 1e-6

N_A = (DEPTH + 2) // 3
N_B = (DEPTH + 1) // 3
N_C = DEPTH // 3
N_DENSE = (DEPTH + 1) // 2
N_MOE = DEPTH // 2

kernel_name = 'hybrid_conv_fourier_sgu_moe_encoder'


def rmsnorm(x, g):
    xf = x.astype(jnp.float32)
    y = xf * lax.rsqrt(jnp.mean(xf * xf, axis=-1, keepdims=True) + EPS)
    return (y * g.astype(jnp.float32)).astype(x.dtype)


def short_conv_mixer(h, w_in, conv_w, w_out):
    seq = h.shape[1]
    b, c, z = jnp.split(h @ w_in, 3, axis=-1)
    cz = jnp.pad(c * z, ((0, 0), (CONV_WIDTH // 2, CONV_WIDTH // 2), (0, 0)))
    conv = sum(cz[:, k:k + seq] * conv_w[k] for k in range(CONV_WIDTH))
    return (b * conv) @ w_out


def fourier_mixer(h, w_out):
    bsz, seq, d = h.shape
    hg = h.astype(jnp.float32).reshape(bsz, seq, FNET_GROUPS, FNET_GROUP_DIM)
    f = jnp.fft.fft2(hg, axes=(1, 3), norm='ortho').real.astype(h.dtype)
    return f.reshape(bsz, seq, d) @ w_out


def spatial_gating_mixer(h, w_in, v_gain, w_s, b_s, w_out):
    bsz, seq, _ = h.shape
    u, v = jnp.split(jax.nn.gelu(h @ w_in, approximate=False), 2, axis=-1)
    v = rmsnorm(v, v_gain)
    vc = v.reshape(bsz, seq // CHUNK, CHUNK, GMLP_HEADS, GMLP_HEAD_DIM)
    sv = jnp.einsum('hts,bnshc->bnthc', w_s, vc) + b_s.T[:, :, None]
    return (u * sv.reshape(bsz, seq, GMLP_HALF)) @ w_out


def swiglu(h, w_gate, w_up, w_down):
    return (jax.nn.silu(h @ w_gate) * (h @ w_up)) @ w_down


def moe_swiglu(h, w_router, w_gate, w_up, w_down):
    logits = h.astype(jnp.float32) @ w_router.astype(jnp.float32)
    top_val, top_idx = lax.top_k(logits, TOP_K)
    gates = jax.nn.softmax(top_val, axis=-1)
    dense_gates = jnp.sum(jax.nn.one_hot(top_idx, N_EXPERTS, dtype=jnp.float32) * gates[..., None], axis=-2)
    dense_gates = dense_gates.astype(h.dtype)
    out = jnp.zeros_like(h)
    for e in range(N_EXPERTS):
        out = out + dense_gates[..., e:e + 1] * swiglu(h, w_gate[e], w_up[e], w_down[e])
    return out


def trunk(x, norm_mix, norm_ffn, norm_final, conv_w_in, conv_w, conv_w_out, fnet_w_out,
          sgu_w_in, sgu_v_gain, sgu_w_s, sgu_b_s, sgu_w_out, ffn_w_gate, ffn_w_up, ffn_w_down,
          moe_w_router, moe_w_gate, moe_w_up, moe_w_down):
    for i in range(DEPTH):
        h = rmsnorm(x, norm_mix[i])
        m, j = i % N_MIXERS, i // N_MIXERS
        if m == 0:
            x = x + short_conv_mixer(h, conv_w_in[j], conv_w[j], conv_w_out[j])
        elif m == 1:
            x = x + fourier_mixer(h, fnet_w_out[j])
        else:
            x = x + spatial_gating_mixer(h, sgu_w_in[j], sgu_v_gain[j], sgu_w_s[j], sgu_b_s[j], sgu_w_out[j])
        h = rmsnorm(x, norm_ffn[i])
        k = i // 2
        if i % 2 == 0:
            x = x + swiglu(h, ffn_w_gate[k], ffn_w_up[k], ffn_w_down[k])
        else:
            x = x + moe_swiglu(h, moe_w_router[k], moe_w_gate[k], moe_w_up[k], moe_w_down[k])
    return rmsnorm(x, norm_final)


def setup_inputs(seed: int = 0) -> dict:
    key = jax.random.key(seed)
    ks = jax.random.split(key, 24)
    f32 = jnp.float32

    def nrm(k, shape, fan_in):
        return jax.random.normal(k, shape, f32) * (fan_in ** -0.5)

    def gain(k, shape):
        return 1.0 + 0.01 * jax.random.normal(k, shape, f32)

    D = D_MODEL
    return {
        'x_prompt': jax.random.normal(ks[0], (BATCH, SEQ, D), f32),
        'x_sample': jax.random.normal(ks[1], (DEC_BATCH, DEC_SEQ, D), f32),
        'norm_mix': gain(ks[2], (DEPTH, D)),
        'norm_ffn': gain(ks[3], (DEPTH, D)),
        'norm_final': gain(ks[4], (D,)),
        'conv_w_in': nrm(ks[5], (N_A, D, 3 * D), D),
        'conv_w': nrm(ks[6], (N_A, CONV_WIDTH, D), CONV_WIDTH),
        'conv_w_out': nrm(ks[7], (N_A, D, D), D),
        'fnet_w_out': nrm(ks[8], (N_B, D, D), D),
        'sgu_w_in': nrm(ks[9], (N_C, D, GMLP_FFN), D),
        'sgu_v_gain': gain(ks[10], (N_C, GMLP_HALF)),
        'sgu_w_s': nrm(ks[11], (N_C, GMLP_HEADS, CHUNK, CHUNK), CHUNK),
        'sgu_b_s': gain(ks[12], (N_C, GMLP_HEADS, CHUNK)),
        'sgu_w_out': nrm(ks[13], (N_C, GMLP_HALF, D), GMLP_HALF),
        'ffn_w_gate': nrm(ks[14], (N_DENSE, D, D_FF), D),
        'ffn_w_up': nrm(ks[15], (N_DENSE, D, D_FF), D),
        'ffn_w_down': nrm(ks[16], (N_DENSE, D_FF, D), D_FF),
        'moe_w_router': nrm(ks[17], (N_MOE, D, N_EXPERTS), D),
        'moe_w_gate': nrm(ks[18], (N_MOE, N_EXPERTS, D, D_FF), D),
        'moe_w_up': nrm(ks[19], (N_MOE, N_EXPERTS, D, D_FF), D),
        'moe_w_down': nrm(ks[20], (N_MOE, N_EXPERTS, D_FF, D), D_FF),
    }


def reference(x_prompt, x_sample, norm_mix, norm_ffn, norm_final, conv_w_in, conv_w, conv_w_out, fnet_w_out,
              sgu_w_in, sgu_v_gain, sgu_w_s, sgu_b_s, sgu_w_out, ffn_w_gate, ffn_w_up, ffn_w_down,
              moe_w_router, moe_w_gate, moe_w_up, moe_w_down):
    params = (norm_mix, norm_ffn, norm_final, conv_w_in, conv_w, conv_w_out, fnet_w_out,
              sgu_w_in, sgu_v_gain, sgu_w_s, sgu_b_s, sgu_w_out, ffn_w_gate, ffn_w_up, ffn_w_down,
              moe_w_router, moe_w_gate, moe_w_up, moe_w_down)
    y_prompt = trunk(x_prompt, *params)
    y_sample = trunk(x_sample, *params)
    return (y_prompt, y_sample)
```

```python
import functools
import math

import jax
import jax.numpy as jnp
from jax import lax
from jax.experimental import pallas as pl
from jax.experimental.pallas import tpu as pltpu

EPS = 1e-6
BF = jnp.bfloat16
F32 = jnp.float32

N_MIXERS = 3
FNET_GROUPS = 8
GMLP_HEADS = 8
CHUNK = 128
TOP_K = 2
LANES = 128
HALO = 16
FNET_N2 = 128


def _cparams(n_axes, vmem_mb):
    return pltpu.CompilerParams(
        dimension_semantics=("arbitrary",) * n_axes,
        vmem_limit_bytes=vmem_mb << 20,
    )


def _rms(x, g):
    ms = jnp.mean(x * x, axis=-1, keepdims=True)
    return x * lax.rsqrt(ms + EPS) * g


def _dot(a, b):
    return jnp.dot(a, b, preferred_element_type=F32)


def _resident(shape, index_map):
    return pl.BlockSpec(shape, index_map, pipeline_mode=pl.Buffered(1))


def _conv_kernel(xp_ref, xc_ref, xn_ref, g_ref, win_ref, cw_ref, wout_ref, o_ref,
                 h_ref, cz_ref, *, tm, t_prompt, s_prompt, s_sample):
    i = pl.program_id(0)
    d = xc_ref.shape[1]
    g = g_ref[...]

    def seq_boundary(r):
        return jnp.where(r <= t_prompt, r % s_prompt == 0, (r - t_prompt) % s_sample == 0)

    first = seq_boundary(i * tm)
    last = seq_boundary(i * tm + tm)
    h_ref[0:HALO, :] = jnp.where(first, 0.0, _rms(xp_ref[...], g)).astype(BF)
    h_ref[HALO:HALO + tm, :] = _rms(xc_ref[...], g).astype(BF)
    h_ref[HALO + tm:, :] = jnp.where(last, 0.0, _rms(xn_ref[...], g)).astype(BF)

    hext = h_ref[...]
    c = _dot(hext, win_ref[:, d:2 * d])
    z = _dot(hext, win_ref[:, 2 * d:])
    cz_ref[...] = c * z
    b = _dot(h_ref[HALO:HALO + tm, :], win_ref[:, :d])
    cw = cw_ref[...]
    conv = (cz_ref[HALO - 1:HALO - 1 + tm, :] * cw[0:1, :]
            + cz_ref[HALO:HALO + tm, :] * cw[1:2, :]
            + cz_ref[HALO + 1:HALO + 1 + tm, :] * cw[2:3, :])
    y = _dot((b * conv).astype(BF), wout_ref[...])
    o_ref[...] = xc_ref[...] + y


def _conv_mixer(x, g, w_in, conv_w, w_out, *, tm, t_prompt, s_prompt, s_sample):
    t, d = x.shape
    nh = tm // HALO
    kern = functools.partial(_conv_kernel, tm=tm, t_prompt=t_prompt,
                             s_prompt=s_prompt, s_sample=s_sample)
    return pl.pallas_call(
        kern,
        grid=(t // tm,),
        in_specs=[
            pl.BlockSpec((HALO, d), lambda i: (jnp.maximum(i * nh - 1, 0), 0)),
            pl.BlockSpec((tm, d), lambda i: (i, 0)),
            pl.BlockSpec((HALO, d), lambda i: (jnp.minimum((i + 1) * nh, t // HALO - 1), 0)),
            _resident((1, d), lambda i: (0, 0)),
            _resident((d, 3 * d), lambda i: (0, 0)),
            _resident((3, d), lambda i: (0, 0)),
            _resident((d, d), lambda i: (0, 0)),
        ],
        out_specs=pl.BlockSpec((tm, d), lambda i: (i, 0)),
        out_shape=jax.ShapeDtypeStruct((t, d), F32),
        scratch_shapes=[pltpu.VMEM((tm + 2 * HALO, d), BF),
                        pltpu.VMEM((tm + 2 * HALO, d), F32)],
        compiler_params=_cparams(1, 48),
        name="conv_mixer",
    )(x, x, x, g.reshape(1, d), w_in.astype(BF), conv_w, w_out.astype(BF))


def _swiglu_kernel(te_ref, nu_ref, x_ref, g_ref, wg_ref, wu_ref, wd_ref, o_ref, *, tf, dense):
    i = pl.program_id(0)

    @pl.when(i < nu_ref[0])
    def _():
        x = x_ref[...]
        h = (_rms(x, g_ref[...]) if dense else x).astype(BF)
        f = wg_ref.shape[-1]
        acc = x if dense else jnp.zeros_like(x)
        for c in range(f // tf):
            a = _dot(h, wg_ref[:, c * tf:(c + 1) * tf])
            b = _dot(h, wu_ref[:, c * tf:(c + 1) * tf])
            act = (jax.nn.silu(a) * b).astype(BF)
            acc = acc + _dot(act, wd_ref[c * tf:(c + 1) * tf, :])
        o_ref[...] = acc

    @pl.when(i >= nu_ref[0])
    def _():
        o_ref[...] = jnp.zeros_like(o_ref)


def _swiglu(x, g, wg, wu, wd, tile_expert, n_used, *, tm, tf, dense):
    t, d = x.shape
    f = wg.shape[-1]
    kern = functools.partial(_swiglu_kernel, tf=tf, dense=dense)
    grid_spec = pltpu.PrefetchScalarGridSpec(
        num_scalar_prefetch=2,
        grid=(t // tm,),
        in_specs=[
            pl.BlockSpec((tm, d), lambda i, te, nu: (i, 0)),
            _resident((1, d), lambda i, te, nu: (0, 0)),
            _resident((None, d, f), lambda i, te, nu: (te[i], 0, 0)),
            _resident((None, d, f), lambda i, te, nu: (te[i], 0, 0)),
            _resident((None, f, d), lambda i, te, nu: (te[i], 0, 0)),
        ],
        out_specs=pl.BlockSpec((tm, d), lambda i, te, nu: (i, 0)),
    )
    return pl.pallas_call(
        kern,
        grid_spec=grid_spec,
        out_shape=jax.ShapeDtypeStruct((t, d), F32),
        compiler_params=_cparams(1, 56),
        name="swiglu_dense" if dense else "swiglu_routed",
    )(tile_expert, n_used, x, g.reshape(1, d), wg, wu, wd)


def _router_kernel(x_ref, g_ref, wr_ref, tri_ref, h_ref, idx_ref, gate_ref, cnt_ref,
                   carry_ref, *, n_exp):
    i = pl.program_id(0)

    @pl.when(i == 0)
    def _():
        carry_ref[...] = jnp.zeros_like(carry_ref)

    h = _rms(x_ref[...], g_ref[...])
    h_ref[...] = h
    logits = jnp.dot(h, wr_ref[...], precision=lax.Precision.HIGHEST,
                     preferred_element_type=F32)
    lane = lax.broadcasted_iota(jnp.int32, logits.shape, 1).astype(F32)
    neg = jnp.float32(-jnp.inf)
    lg = jnp.where(lane < n_exp, logits, neg)
    v1 = jnp.max(lg, axis=1, keepdims=True)
    i1 = jnp.min(jnp.where(lg == v1, lane, float(LANES)), axis=1, keepdims=True)
    lg2 = jnp.where(lane == i1, neg, lg)
    v2 = jnp.max(lg2, axis=1, keepdims=True)
    i2 = jnp.min(jnp.where(lg2 == v2, lane, float(LANES)), axis=1, keepdims=True)
    e2 = jnp.exp(v2 - v1)
    g1 = 1.0 / (1.0 + e2)
    g2 = e2 / (1.0 + e2)

    sel = jnp.where((lane == i1) | (lane == i2), 1.0, 0.0)
    cum = _dot(tri_ref[...], sel.astype(BF)) + carry_ref[...]
    r1 = jnp.sum(jnp.where(lane == i1, cum, 0.0), axis=1, keepdims=True)
    r2 = jnp.sum(jnp.where(lane == i2, cum, 0.0), axis=1, keepdims=True)
    carry_ref[...] = carry_ref[...] + jnp.sum(sel, axis=0, keepdims=True)
    cnt_ref[...] = carry_ref[...]

    idx_ref[...] = jnp.where(lane == 0, i1, jnp.where(lane == 1, i2, jnp.where(
        lane == 2, r1, jnp.where(lane == 3, r2, 0.0)))).astype(jnp.int32)
    gate_ref[...] = jnp.where(lane == 0, g1, jnp.where(lane == 1, g2, 0.0))


def _router(x, g, w_router, *, tm):
    t, d = x.shape
    n_exp = w_router.shape[1]
    wr = jnp.zeros((d, LANES), F32).at[:, :n_exp].set(w_router.astype(F32))
    row = lax.broadcasted_iota(jnp.int32, (tm, tm), 0)
    col = lax.broadcasted_iota(jnp.int32, (tm, tm), 1)
    tri = (col < row).astype(BF)
    kern = functools.partial(_router_kernel, n_exp=n_exp)
    return pl.pallas_call(
        kern,
        grid=(t // tm,),
        in_specs=[
            pl.BlockSpec((tm, d), lambda i: (i, 0)),
            _resident((1, d), lambda i: (0, 0)),
            _resident((d, LANES), lambda i: (0, 0)),
            _resident((tm, tm), lambda i: (0, 0)),
        ],
        out_specs=[
            pl.BlockSpec((tm, d), lambda i: (i, 0)),
            pl.BlockSpec((tm, LANES), lambda i: (i, 0)),
            pl.BlockSpec((tm, LANES), lambda i: (i, 0)),
            pl.BlockSpec((1, LANES), lambda i: (0, 0)),
        ],
        out_shape=[
            jax.ShapeDtypeStruct((t, d), F32),
            jax.ShapeDtypeStruct((t, LANES), jnp.int32),
            jax.ShapeDtypeStruct((t, LANES), F32),
            jax.ShapeDtypeStruct((1, LANES), F32),
        ],
        scratch_shapes=[pltpu.VMEM((1, LANES), F32)],
        compiler_params=_cparams(1, 32),
        name="router",
    )(x, g.reshape(1, d), wr, tri)


def _dispatch_kernel(pos_ref, h_ref, xs_in_ref, xs_ref, sem):
    del xs_in_ref
    tm = h_ref.shape[0]

    def issue(r, carry):
        for k in range(TOP_K):
            p = pos_ref[0, 0, k * tm + r]
            pltpu.make_async_copy(h_ref.at[pl.ds(r, 1), :], xs_ref.at[pl.ds(p, 1), :], sem).start()
        return carry

    lax.fori_loop(0, tm, issue, 0)
    for k in range(TOP_K):
        pltpu.make_async_copy(h_ref, xs_ref.at[pl.ds(0, tm), :], sem).wait()


def _dispatch(h, pos3, n_rows, *, tm):
    t, d = h.shape
    xs0 = jnp.zeros((n_rows, d), F32)
    return pl.pallas_call(
        _dispatch_kernel,
        grid=(t // tm,),
        in_specs=[
            pl.BlockSpec((1, 1, TOP_K * tm), lambda i: (i, 0, 0), memory_space=pltpu.SMEM),
            pl.BlockSpec((tm, d), lambda i: (i, 0)),
            pl.BlockSpec(memory_space=pl.ANY),
        ],
        out_specs=pl.BlockSpec(memory_space=pl.ANY),
        out_shape=jax.ShapeDtypeStruct((n_rows, d), F32),
        scratch_shapes=[pltpu.SemaphoreType.DMA(())],
        input_output_aliases={2: 0},
        compiler_params=_cparams(1, 32),
        name="dispatch",
    )(pos3, h, xs0)


def _combine_kernel(pos_ref, x_ref, gate_ref, gf_ref, ys_ref, o_ref, buf_ref, sem, *, final_norm):
    tm = x_ref.shape[0]

    def issue(r, carry):
        for k in range(TOP_K):
            p = pos_ref[0, 0, k * tm + r]
            pltpu.make_async_copy(ys_ref.at[pl.ds(p, 1), :], buf_ref.at[k, pl.ds(r, 1), :], sem).start()
        return carry

    lax.fori_loop(0, tm, issue, 0)
    for k in range(TOP_K):
        pltpu.make_async_copy(ys_ref.at[pl.ds(0, tm), :], buf_ref.at[k], sem).wait()
    gate = gate_ref[...]
    y = x_ref[...] + gate[:, 0:1] * buf_ref[0] + gate[:, 1:2] * buf_ref[1]
    if final_norm:
        y = _rms(y, gf_ref[...])
    o_ref[...] = y


def _combine(x, gates, pos3, ys, g_final, *, tm, final_norm):
    t, d = x.shape
    kern = functools.partial(_combine_kernel, final_norm=final_norm)
    return pl.pallas_call(
        kern,
        grid=(t // tm,),
        in_specs=[
            pl.BlockSpec((1, 1, TOP_K * tm), lambda i: (i, 0, 0), memory_space=pltpu.SMEM),
            pl.BlockSpec((tm, d), lambda i: (i, 0)),
            pl.BlockSpec((tm, LANES), lambda i: (i, 0)),
            _resident((1, d), lambda i: (0, 0)),
            pl.BlockSpec(memory_space=pl.ANY),
        ],
        out_specs=pl.BlockSpec((tm, d), lambda i: (i, 0)),
        out_shape=jax.ShapeDtypeStruct((t, d), F32),
        scratch_shapes=[pltpu.VMEM((TOP_K, tm, d), F32), pltpu.SemaphoreType.DMA(())],
        compiler_params=_cparams(1, 32),
        name="combine",
    )(pos3, x, gates, g_final.reshape(1, d), ys)


def _moe_layer(x, g, w_router, wg, wu, wd, g_final, *, tm_r, tm_e, tm_dc, tf, final_norm):
    t, d = x.shape
    n_exp = w_router.shape[1]
    h, idx, gates, cnt = _router(x, g, w_router, tm=tm_r)

    counts = cnt[0, :n_exp].astype(jnp.int32)
    padded = ((counts + tm_e - 1) // tm_e) * tm_e
    ends = jnp.cumsum(padded)
    base = ends - padded
    experts = jnp.arange(n_exp, dtype=jnp.int32)

    def positions(e, r):
        return jnp.sum(jnp.where(e[:, None] == experts[None, :], base[None, :], 0), axis=1) + r

    pos = jnp.stack([positions(idx[:, 0], idx[:, 2]), positions(idx[:, 1], idx[:, 3])], axis=0)
    pos3 = pos.reshape(TOP_K, t // tm_dc, tm_dc).transpose(1, 0, 2).reshape(t // tm_dc, 1, TOP_K * tm_dc)

    n_rows = TOP_K * t + n_exp * tm_e
    n_tiles = n_rows // tm_e
    tile_start = jnp.arange(n_tiles, dtype=jnp.int32) * tm_e
    tile_expert = jnp.minimum(
        jnp.sum((tile_start[:, None] >= ends[None, :]).astype(jnp.int32), axis=1), n_exp - 1)
    n_used = (ends[-1:] // tm_e).astype(jnp.int32)

    xs = _dispatch(h, pos3, n_rows, tm=tm_dc)
    ys = _swiglu(xs, g, wg, wu, wd, tile_expert, n_used, tm=tm_e, tf=tf, dense=False)
    return _combine(x, gates, pos3, ys, g_final, tm=tm_dc, final_norm=final_norm)


def _fnet_fold_kernel(cs_ref, w_ref, o_ref):
    y = jnp.dot(cs_ref[...], w_ref[...], precision=lax.Precision.HIGHEST,
                preferred_element_type=F32)
    gd = w_ref.shape[0]
    o_ref[0, 0] = y[:gd].astype(BF)
    o_ref[1, 0] = y[gd:].astype(BF)


def _fnet_fold(w_out):
    d = w_out.shape[0]
    gd = d // FNET_GROUPS
    kk = (jnp.arange(gd, dtype=jnp.int32)[:, None] * jnp.arange(gd, dtype=jnp.int32)[None, :]) % gd
    ang = kk.astype(F32) * (2.0 * math.pi / gd)
    cs = jnp.concatenate([jnp.cos(ang), -jnp.sin(ang)], axis=0) * (gd ** -0.5)
    out = pl.pallas_call(
        _fnet_fold_kernel,
        grid=(FNET_GROUPS,),
        in_specs=[_resident((2 * gd, gd), lambda i: (0, 0)),
                  pl.BlockSpec((gd, d), lambda i: (i, 0))],
        out_specs=pl.BlockSpec((2, 1, gd, d), lambda i: (0, i, 0, 0)),
        out_shape=jax.ShapeDtypeStruct((2, FNET_GROUPS, gd, d), BF),
        compiler_params=_cparams(1, 32),
        name="fnet_fold",
    )(cs, w_out)
    return out.reshape(2 * d, d)


def _fnet1_kernel(x_ref, g_ref, m1_ref, z_ref, *, nb, d):
    g = g_ref[...]
    for j in range(nb):
        h = _rms(x_ref[:, j * d:(j + 1) * d], g).astype(BF)
        z_ref[0, :, j * d:(j + 1) * d] = _dot(m1_ref[j], h).astype(BF)


def _fnet2_kernel(z_ref, x_ref, m3_ref, wf_ref, o_ref, ab_ref, *, kb, d):
    n2 = x_ref.shape[0]
    m3 = m3_ref[...]
    for j in range(kb):
        ab = _dot(m3, z_ref[0, j].reshape(2 * n2, d))
        ab_ref[j * n2:(j + 1) * n2, :d] = ab[:n2].astype(BF)
        ab_ref[j * n2:(j + 1) * n2, d:] = ab[n2:].astype(BF)
    y = _dot(ab_ref[...], wf_ref[...])
    for j in range(kb):
        o_ref[:, j * d:(j + 1) * d] = x_ref[:, j * d:(j + 1) * d] + y[j * n2:(j + 1) * n2]


def _fnet_group(x, g, wfold, *, row0, bsz, seq, nb, kb):
    t, d = x.shape
    n2 = FNET_N2
    n1 = seq // n2
    k1 = jnp.arange(n1, dtype=jnp.int32)
    nn = n2 * jnp.arange(n1, dtype=jnp.int32)[None, None, :] + jnp.arange(n2, dtype=jnp.int32)[:, None, None]
    ang1 = ((k1[None, :, None] * nn) % seq).astype(F32) * (2.0 * math.pi / seq)
    m1 = (jnp.stack([jnp.cos(ang1), jnp.sin(ang1)], axis=2) * (n1 ** -0.5)).reshape(n2, 2 * n1, n1).astype(BF)
    q = jnp.arange(n2, dtype=jnp.int32)
    ang3 = ((q[:, None] * q[None, :]) % n2).astype(F32) * (2.0 * math.pi / n2)
    c3, s3 = jnp.cos(ang3), jnp.sin(ang3)
    m3 = (jnp.concatenate([jnp.concatenate([c3, -s3], axis=1),
                           jnp.concatenate([s3, c3], axis=1)], axis=0) * (n2 ** -0.5)).astype(BF)

    rb = row0 // seq
    xv = x.reshape(t // n2, n2 * d)
    z = pl.pallas_call(
        functools.partial(_fnet1_kernel, nb=nb, d=d),
        grid=(bsz, n2 // nb),
        in_specs=[
            pl.BlockSpec((n1, nb * d), lambda b, j: (rb + b, j)),
            _resident((1, d), lambda b, j: (0, 0)),
            pl.BlockSpec((nb, 2 * n1, n1), lambda b, j: (j, 0, 0)),
        ],
        out_specs=pl.BlockSpec((1, 2 * n1, nb * d), lambda b, j: (b, 0, j)),
        out_shape=jax.ShapeDtypeStruct((bsz, 2 * n1, n2 * d), BF),
        compiler_params=_cparams(2, 48),
        name="fnet_stage1",
    )(xv, g.reshape(1, d), m1)

    z5 = z.reshape(bsz, n1, 2, n2, d)
    xo = x.reshape(t // n1, n1 * d)
    out = pl.pallas_call(
        functools.partial(_fnet2_kernel, kb=kb, d=d),
        grid=(bsz, n1 // kb),
        in_specs=[
            pl.BlockSpec((1, kb, 2, n2, d), lambda b, j: (b, j, 0, 0, 0)),
            pl.BlockSpec((n2, kb * d), lambda b, j: (rb + b, j)),
            _resident((2 * n2, 2 * n2), lambda b, j: (0, 0)),
            _resident((2 * d, d), lambda b, j: (0, 0)),
        ],
        out_specs=pl.BlockSpec((n2, kb * d), lambda b, j: (rb + b, j)),
        out_shape=jax.ShapeDtypeStruct((t // n1, n1 * d), F32),
        scratch_shapes=[pltpu.VMEM((kb * n2, 2 * d), BF)],
        input_output_aliases={1: 0},
        compiler_params=_cparams(2, 48),
        name="fnet_stage2",
    )(z5, xo, m3, wfold)
    return out.reshape(t, d)


def _sgu_kernel(x_ref, g_ref, win_ref, vg_ref, ws_ref, bs_ref, wout_ref, o_ref, v_ref, sv_ref, *, tf):
    tm = x_ref.shape[0]
    half = vg_ref.shape[1]
    hd = half // GMLP_HEADS
    x = x_ref[...]
    h = _rms(x, g_ref[...]).astype(BF)
    sqrt_half = math.sqrt(0.5)

    def gelu(a):
        return 0.5 * a * (1.0 + lax.erf(a * sqrt_half))

    ssq = jnp.zeros((tm, 1), F32)
    for c in range(half // tf):
        vc = gelu(_dot(h, win_ref[:, half + c * tf:half + (c + 1) * tf]))
        sv_ref[:, c * tf:(c + 1) * tf] = vc
        ssq = ssq + jnp.sum(vc * vc, axis=-1, keepdims=True)
    inv = lax.rsqrt(ssq * (1.0 / half) + EPS)
    v_ref[...] = (sv_ref[...] * inv * vg_ref[...]).astype(BF)
    for r in range(tm // CHUNK):
        for hh in range(GMLP_HEADS):
            blk = _dot(ws_ref[hh], v_ref[r * CHUNK:(r + 1) * CHUNK, hh * hd:(hh + 1) * hd])
            sv_ref[r * CHUNK:(r + 1) * CHUNK, hh * hd:(hh + 1) * hd] = blk + bs_ref[hh][:, 0:1]
    acc = x
    for c in range(half // tf):
        uc = gelu(_dot(h, win_ref[:, c * tf:(c + 1) * tf]))
        acc = acc + _dot((uc * sv_ref[:, c * tf:(c + 1) * tf]).astype(BF),
                         wout_ref[c * tf:(c + 1) * tf, :])
    o_ref[...] = acc


def _sgu_mixer(x, g, w_in, v_gain, w_s, b_s, w_out, *, tm, tf):
    t, d = x.shape
    half = v_gain.shape[0]
    bs = jnp.broadcast_to(b_s[:, :, None], (GMLP_HEADS, CHUNK, LANES)).astype(F32)
    return pl.pallas_call(
        functools.partial(_sgu_kernel, tf=tf),
        grid=(t // tm,),
        in_specs=[
            pl.BlockSpec((tm, d), lambda i: (i, 0)),
            _resident((1, d), lambda i: (0, 0)),
            _resident((d, 2 * half), lambda i: (0, 0)),
            _resident((1, half), lambda i: (0, 0)),
            _resident((GMLP_HEADS, CHUNK, CHUNK), lambda i: (0, 0, 0)),
            _resident((GMLP_HEADS, CHUNK, LANES), lambda i: (0, 0, 0)),
            _resident((half, d), lambda i: (0, 0)),
        ],
        out_specs=pl.BlockSpec((tm, d), lambda i: (i, 0)),
        out_shape=jax.ShapeDtypeStruct((t, d), F32),
        scratch_shapes=[pltpu.VMEM((tm, half), BF), pltpu.VMEM((tm, half), F32)],
        compiler_params=_cparams(1, 56),
        name="sgu_mixer",
    )(x, g.reshape(1, d), w_in.astype(BF), v_gain.reshape(1, half), w_s.astype(BF), bs,
      w_out.astype(BF))


def _final_norm_kernel(x_ref, g_ref, o_ref):
    o_ref[...] = _rms(x_ref[...], g_ref[...])


def _final_norm(x, g, *, tm):
    t, d = x.shape
    return pl.pallas_call(
        _final_norm_kernel,
        grid=(t // tm,),
        in_specs=[pl.BlockSpec((tm, d), lambda i: (i, 0)), _resident((1, d), lambda i: (0, 0))],
        out_specs=pl.BlockSpec((tm, d), lambda i: (i, 0)),
        out_shape=jax.ShapeDtypeStruct((t, d), F32),
        compiler_params=_cparams(1, 32),
        name="final_norm",
    )(x, g.reshape(1, d))


def _tile(n, want):
    while n % want:
        want //= 2
    return want


def kernel(x_prompt, x_sample, norm_mix, norm_ffn, norm_final, conv_w_in, conv_w, conv_w_out, fnet_w_out, sgu_w_in, sgu_v_gain, sgu_w_s, sgu_b_s, sgu_w_out, ffn_w_gate, ffn_w_up, ffn_w_down, moe_w_router, moe_w_gate, moe_w_up, moe_w_down):
    bp, sp, d = x_prompt.shape
    bs_, ss, _ = x_sample.shape
    t_prompt = bp * sp
    t = t_prompt + bs_ * ss
    depth = norm_mix.shape[0]
    f = ffn_w_gate.shape[-1]
    tf = f // 7 if f % 7 == 0 else _tile(f, 512)
    tm = _tile(math.gcd(sp, ss), 512)
    half = sgu_v_gain.shape[-1]

    x = jnp.concatenate([x_prompt.reshape(t_prompt, d), x_sample.reshape(t - t_prompt, d)], axis=0)
    zero1 = jnp.zeros((t // tm,), jnp.int32)
    all_tiles = jnp.full((1,), t // tm, jnp.int32)

    for i in range(depth):
        m, j = i % N_MIXERS, i // N_MIXERS
        if m == 0:
            x = _conv_mixer(x, norm_mix[i], conv_w_in[j], conv_w[j], conv_w_out[j], tm=tm,
                            t_prompt=t_prompt, s_prompt=sp, s_sample=ss)
        elif m == 1:
            wfold = _fnet_fold(fnet_w_out[j])
            x = _fnet_group(x, norm_mix[i], wfold, row0=0, bsz=bp, seq=sp, nb=4, kb=4)
            x = _fnet_group(x, norm_mix[i], wfold, row0=t_prompt, bsz=bs_, seq=ss, nb=4, kb=4)
        else:
            x = _sgu_mixer(x, norm_mix[i], sgu_w_in[j], sgu_v_gain[j], sgu_w_s[j], sgu_b_s[j],
                           sgu_w_out[j], tm=_tile(tm, 256), tf=_tile(half, 512))
        k = i // 2
        if i % 2 == 0:
            x = _swiglu(x, norm_ffn[i], ffn_w_gate[k][None].astype(BF), ffn_w_up[k][None].astype(BF),
                        ffn_w_down[k][None].astype(BF), zero1, all_tiles, tm=tm, tf=tf, dense=True)
            if i == depth - 1:
                x = _final_norm(x, norm_final, tm=tm)
        else:
            x = _moe_layer(x, norm_ffn[i], moe_w_router[k], moe_w_gate[k].astype(BF),
                           moe_w_up[k].astype(BF), moe_w_down[k].astype(BF), norm_final,
                           tm_r=tm, tm_e=tm, tm_dc=tm, tf=tf, final_norm=(i == depth - 1))
    return (x[:t_prompt].reshape(bp, sp, d), x[t_prompt:].reshape(bs_, ss, d))
```

```python
import functools
import math

import jax
import jax.numpy as jnp
from jax import lax
from jax.experimental import pallas as pl
from jax.experimental.pallas import tpu as pltpu

EPS = 1e-6
BF = jnp.bfloat16
F32 = jnp.float32

N_MIXERS = 3
FNET_GROUPS = 8
GMLP_HEADS = 8
CHUNK = 128
TOP_K = 2
LANES = 128
SUBLANES = 8
HALO = 16
FNET_N2 = 128
FNET_ROWS = 1024


def _cparams(n_axes, vmem_mb):
    return pltpu.CompilerParams(
        dimension_semantics=("arbitrary",) * n_axes,
        vmem_limit_bytes=vmem_mb << 20,
    )


def _rms(x, g):
    ms = jnp.mean(x * x, axis=-1, keepdims=True)
    return x * lax.rsqrt(ms + EPS) * g


def _dot(a, b):
    return jnp.dot(a, b, preferred_element_type=F32)


def _resident(shape, index_map):
    return pl.BlockSpec(shape, index_map, pipeline_mode=pl.Buffered(1))


def _conv_kernel(*refs, tm, t_prompt, s_prompt, s_sample, two_sources):
    g_ref, win_ref, cw_ref, wout_ref, o_ref, h_ref, cz_ref = refs[-7:]
    i = pl.program_id(0)
    d = o_ref.shape[1]
    g = g_ref[...]
    if two_sources:
        in_prompt = i * tm < t_prompt
        xp, xc, xn = (jnp.where(in_prompt, a[...], b[...]) for a, b in zip(refs[0:3], refs[3:6]))
    else:
        xp, xc, xn = (a[...] for a in refs[0:3])

    def seq_boundary(r):
        return jnp.where(r <= t_prompt, r % s_prompt == 0, (r - t_prompt) % s_sample == 0)

    first = seq_boundary(i * tm)
    last = seq_boundary(i * tm + tm)
    h_ref[0:HALO, :] = jnp.where(first, 0.0, _rms(xp, g)).astype(BF)
    h_ref[HALO:HALO + tm, :] = _rms(xc, g).astype(BF)
    h_ref[HALO + tm:, :] = jnp.where(last, 0.0, _rms(xn, g)).astype(BF)

    hext = h_ref[...]
    c = _dot(hext, win_ref[:, d:2 * d])
    z = _dot(hext, win_ref[:, 2 * d:])
    cz_ref[...] = c * z
    b = _dot(h_ref[HALO:HALO + tm, :], win_ref[:, :d])
    cw = cw_ref[...]
    conv = (cz_ref[HALO - 1:HALO - 1 + tm, :] * cw[0:1, :]
            + cz_ref[HALO:HALO + tm, :] * cw[1:2, :]
            + cz_ref[HALO + 1:HALO + 1 + tm, :] * cw[2:3, :])
    y = _dot((b * conv).astype(BF), wout_ref[...])
    o_ref[...] = xc + y


def _conv_mixer(x, x_sample, g, w_in, conv_w, w_out, *, tm, t_prompt, s_prompt, s_sample):
    d = x.shape[1]
    xs = [x] if x_sample is None else [x, x_sample]
    t = sum(a.shape[0] for a in xs)
    nh = tm // HALO
    kern = functools.partial(_conv_kernel, tm=tm, t_prompt=t_prompt, s_prompt=s_prompt,
                             s_sample=s_sample, two_sources=len(xs) == 2)

    def source_specs(rows, tile0):
        nt, nhb = rows // tm, rows // HALO
        cur = lambda i: jnp.clip(i - tile0, 0, nt - 1)
        return [
            pl.BlockSpec((HALO, d), lambda i: (jnp.maximum(cur(i) * nh - 1, 0), 0)),
            pl.BlockSpec((tm, d), lambda i: (cur(i), 0)),
            pl.BlockSpec((HALO, d), lambda i: (jnp.minimum((cur(i) + 1) * nh, nhb - 1), 0)),
        ]

    if len(xs) == 2:
        in_specs = source_specs(t_prompt, 0) + source_specs(t - t_prompt, t_prompt // tm)
        args = [xs[0]] * 3 + [xs[1]] * 3
    else:
        in_specs = source_specs(t, 0)
        args = [xs[0]] * 3
    in_specs += [
        _resident((1, d), lambda i: (0, 0)),
        _resident((d, 3 * d), lambda i: (0, 0)),
        _resident((3, d), lambda i: (0, 0)),
        _resident((d, d), lambda i: (0, 0)),
    ]
    args += [g.reshape(1, d), w_in.astype(BF), conv_w, w_out.astype(BF)]
    return pl.pallas_call(
        kern,
        grid=(t // tm,),
        in_specs=in_specs,
        out_specs=pl.BlockSpec((tm, d), lambda i: (i, 0)),
        out_shape=jax.ShapeDtypeStruct((t, d), F32),
        scratch_shapes=[pltpu.VMEM((tm + 2 * HALO, d), BF),
                        pltpu.VMEM((tm + 2 * HALO, d), F32)],
        compiler_params=_cparams(1, 48),
        name="conv_mixer",
    )(*args)


def _swiglu_kernel(te_ref, nu_ref, x_ref, g_ref, wg_ref, wu_ref, wd_ref, o_ref, *, tf, dense):
    i = pl.program_id(0)

    @pl.when(i < nu_ref[0])
    def _():
        x = x_ref[...]
        h = (_rms(x, g_ref[...]) if dense else x).astype(BF)
        f = wg_ref.shape[-1]
        acc = x if dense else jnp.zeros_like(x)
        for c in range(f // tf):
            a = _dot(h, wg_ref[:, c * tf:(c + 1) * tf])
            b = _dot(h, wu_ref[:, c * tf:(c + 1) * tf])
            act = (jax.nn.silu(a) * b).astype(BF)
            acc = acc + _dot(act, wd_ref[c * tf:(c + 1) * tf, :])
        o_ref[...] = acc

    @pl.when(i >= nu_ref[0])
    def _():
        o_ref[...] = jnp.zeros_like(o_ref)


def _swiglu(x, g, wg, wu, wd, layer, tile_expert, n_used, *, tm, tf, dense):
    t, d = x.shape
    f = wg.shape[-1]
    kern = functools.partial(_swiglu_kernel, tf=tf, dense=dense)
    grid_spec = pltpu.PrefetchScalarGridSpec(
        num_scalar_prefetch=2,
        grid=(t // tm,),
        in_specs=[
            pl.BlockSpec((tm, d), lambda i, te, nu: (i, 0)),
            _resident((1, d), lambda i, te, nu: (0, 0)),
            _resident((None, None, d, f), lambda i, te, nu: (layer, te[i], 0, 0)),
            _resident((None, None, d, f), lambda i, te, nu: (layer, te[i], 0, 0)),
            _resident((None, None, f, d), lambda i, te, nu: (layer, te[i], 0, 0)),
        ],
        out_specs=pl.BlockSpec((tm, d), lambda i, te, nu: (i, 0)),
    )
    return pl.pallas_call(
        kern,
        grid_spec=grid_spec,
        out_shape=jax.ShapeDtypeStruct((t, d), F32),
        compiler_params=_cparams(1, 56),
        name="swiglu_dense" if dense else "swiglu_routed",
    )(tile_expert, n_used, x, g.reshape(1, d), wg, wu, wd)


def _router_kernel(x_ref, g_ref, wr_ref, tri_ref, h_ref, meta_ref, gate_ref, cnt_ref,
                   carry_ref, *, n_exp):
    i = pl.program_id(0)

    @pl.when(i == 0)
    def _():
        carry_ref[...] = jnp.zeros_like(carry_ref)

    h = _rms(x_ref[...], g_ref[...])
    h_ref[...] = h
    logits = jnp.dot(h, wr_ref[...], precision=lax.Precision.HIGHEST,
                     preferred_element_type=F32)
    lane = lax.broadcasted_iota(jnp.int32, logits.shape, 1).astype(F32)
    neg = jnp.float32(-jnp.inf)
    lg = jnp.where(lane < n_exp, logits, neg)
    v1 = jnp.max(lg, axis=1, keepdims=True)
    i1 = jnp.min(jnp.where(lg == v1, lane, float(LANES)), axis=1, keepdims=True)
    lg2 = jnp.where(lane == i1, neg, lg)
    v2 = jnp.max(lg2, axis=1, keepdims=True)
    i2 = jnp.min(jnp.where(lg2 == v2, lane, float(LANES)), axis=1, keepdims=True)
    e2 = jnp.exp(v2 - v1)
    g1 = 1.0 / (1.0 + e2)
    g2 = e2 / (1.0 + e2)

    sel = jnp.where((lane == i1) | (lane == i2), 1.0, 0.0)
    cum = _dot(tri_ref[...], sel.astype(BF)) + carry_ref[...]
    r1 = jnp.sum(jnp.where(lane == i1, cum, 0.0), axis=1, keepdims=True)
    r2 = jnp.sum(jnp.where(lane == i2, cum, 0.0), axis=1, keepdims=True)
    carry_ref[...] = carry_ref[...] + jnp.sum(sel, axis=0, keepdims=True)
    cnt_ref[...] = carry_ref[...]

    meta = jnp.where(lane == 0, i1, jnp.where(lane == 1, i2, jnp.where(
        lane == 2, r1, jnp.where(lane == 3, r2, 0.0))))
    meta_ref[...] = meta.T[0:SUBLANES, :]
    gate_ref[...] = jnp.where(lane == 0, g1, jnp.where(lane == 1, g2, 0.0))


def _router(x, g, w_router, *, tm):
    t, d = x.shape
    n_exp = w_router.shape[1]
    wr = jnp.zeros((d, LANES), F32).at[:, :n_exp].set(w_router.astype(F32))
    row = lax.broadcasted_iota(jnp.int32, (tm, tm), 0)
    col = lax.broadcasted_iota(jnp.int32, (tm, tm), 1)
    tri = (col < row).astype(BF)
    kern = functools.partial(_router_kernel, n_exp=n_exp)
    return pl.pallas_call(
        kern,
        grid=(t // tm,),
        in_specs=[
            pl.BlockSpec((tm, d), lambda i: (i, 0)),
            _resident((1, d), lambda i: (0, 0)),
            _resident((d, LANES), lambda i: (0, 0)),
            _resident((tm, tm), lambda i: (0, 0)),
        ],
        out_specs=[
            pl.BlockSpec((tm, d), lambda i: (i, 0)),
            pl.BlockSpec((SUBLANES, tm), lambda i: (0, i)),
            pl.BlockSpec((tm, LANES), lambda i: (i, 0)),
            pl.BlockSpec((1, LANES), lambda i: (0, 0)),
        ],
        out_shape=[
            jax.ShapeDtypeStruct((t, d), F32),
            jax.ShapeDtypeStruct((SUBLANES, t), F32),
            jax.ShapeDtypeStruct((t, LANES), F32),
            jax.ShapeDtypeStruct((1, LANES), F32),
        ],
        scratch_shapes=[pltpu.VMEM((1, LANES), F32)],
        compiler_params=_cparams(1, 32),
        name="router",
    )(x, g.reshape(1, d), wr, tri)


def _dispatch_kernel(zs_ref, pos_ref, h_ref, xs_ref, zero_ref, sem, zsem, *, n_exp):
    tm = h_ref.shape[0]

    @pl.when(pl.program_id(0) == 0)
    def _():
        zero_ref[...] = jnp.zeros_like(zero_ref)
        nz = zero_ref.shape[0]
        n_rows = xs_ref.shape[0]
        for e in range(n_exp):
            start = pl.multiple_of(zs_ref[e], SUBLANES)
            cp = pltpu.make_async_copy(zero_ref, xs_ref.at[pl.ds(start, nz), :], zsem)
            cp.start()
            cp.wait()
            cp = pltpu.make_async_copy(zero_ref.at[pl.ds(0, tm), :],
                                       xs_ref.at[pl.ds(n_rows - (e + 1) * tm, tm), :], zsem)
            cp.start()
            cp.wait()

    for r in range(tm):
        for k in range(TOP_K):
            p = pos_ref[0, 0, k * tm + r]
            pltpu.make_async_copy(h_ref.at[pl.ds(r, 1), :], xs_ref.at[pl.ds(p, 1), :], sem).start()
    for k in range(TOP_K):
        pltpu.make_async_copy(h_ref, xs_ref.at[pl.ds(0, tm), :], sem).wait()


def _dispatch(h, pos3, zero_start, n_rows, *, tm):
    t, d = h.shape
    n_exp = zero_start.shape[0]
    grid_spec = pltpu.PrefetchScalarGridSpec(
        num_scalar_prefetch=1,
        grid=(t // tm,),
        in_specs=[
            pl.BlockSpec((1, 1, TOP_K * tm), lambda i, zs: (i, 0, 0), memory_space=pltpu.SMEM),
            pl.BlockSpec((tm, d), lambda i, zs: (i, 0)),
        ],
        out_specs=pl.BlockSpec(memory_space=pl.ANY),
        scratch_shapes=[pltpu.VMEM((tm + SUBLANES, d), F32), pltpu.SemaphoreType.DMA(()),
                        pltpu.SemaphoreType.DMA(())],
    )
    return pl.pallas_call(
        functools.partial(_dispatch_kernel, n_exp=n_exp),
        grid_spec=grid_spec,
        out_shape=jax.ShapeDtypeStruct((n_rows, d), F32),
        compiler_params=_cparams(1, 32),
        name="dispatch",
    )(zero_start, pos3, h)


def _gather_rows(pos_ref, ys_ref, buf_ref, sem, slot, tm):
    for r in range(tm):
        for k in range(TOP_K):
            p = pos_ref[0, 0, k * tm + r]
            pltpu.make_async_copy(ys_ref.at[pl.ds(p, 1), :], buf_ref.at[slot, k, pl.ds(r, 1), :],
                                  sem.at[slot]).start()


def _combine_kernel(pos_ref, posn_ref, x_ref, gate_ref, gf_ref, ys_ref, *rest, n_prompt_tiles):
    if n_prompt_tiles is None:
        (o_ref,), (buf_ref, sem) = rest[:1], rest[1:]
    else:
        (op_ref, os_ref), (buf_ref, sem) = rest[:2], rest[2:]
    i = pl.program_id(0)
    n = pl.num_programs(0)
    tm = x_ref.shape[0]
    slot = i % 2

    @pl.when(i == 0)
    def _():
        _gather_rows(pos_ref, ys_ref, buf_ref, sem, 0, tm)

    @pl.when(i + 1 < n)
    def _():
        _gather_rows(posn_ref, ys_ref, buf_ref, sem, 1 - slot, tm)

    for k in range(TOP_K):
        pltpu.make_async_copy(ys_ref.at[pl.ds(0, tm), :], buf_ref.at[slot, k], sem.at[slot]).wait()
    gate = gate_ref[...]
    y = x_ref[...] + gate[:, 0:1] * buf_ref[slot, 0] + gate[:, 1:2] * buf_ref[slot, 1]
    if n_prompt_tiles is None:
        o_ref[...] = y
    else:
        y = _rms(y, gf_ref[...])

        @pl.when(i < n_prompt_tiles)
        def _():
            op_ref[...] = y

        @pl.when(i >= n_prompt_tiles)
        def _():
            os_ref[...] = y


def _combine(x, gates, pos3, ys, g_final, *, tm, t_prompt=None):
    t, d = x.shape
    nt = t // tm
    if t_prompt is None:
        npt = None
        out_specs = pl.BlockSpec((tm, d), lambda i: (i, 0))
        out_shape = jax.ShapeDtypeStruct((t, d), F32)
    else:
        npt = t_prompt // tm
        out_specs = [pl.BlockSpec((tm, d), lambda i: (jnp.minimum(i, npt - 1), 0)),
                     pl.BlockSpec((tm, d), lambda i: (jnp.maximum(i - npt, 0), 0))]
        out_shape = [jax.ShapeDtypeStruct((t_prompt, d), F32),
                     jax.ShapeDtypeStruct((t - t_prompt, d), F32)]
    return pl.pallas_call(
        functools.partial(_combine_kernel, n_prompt_tiles=npt),
        grid=(nt,),
        in_specs=[
            pl.BlockSpec((1, 1, TOP_K * tm), lambda i: (i, 0, 0), memory_space=pltpu.SMEM),
            pl.BlockSpec((1, 1, TOP_K * tm), lambda i: (jnp.minimum(i + 1, nt - 1), 0, 0),
                         memory_space=pltpu.SMEM),
            pl.BlockSpec((tm, d), lambda i: (i, 0)),
            pl.BlockSpec((tm, LANES), lambda i: (i, 0)),
            _resident((1, d), lambda i: (0, 0)),
            pl.BlockSpec(memory_space=pl.ANY),
        ],
        out_specs=out_specs,
        out_shape=out_shape,
        scratch_shapes=[pltpu.VMEM((2, TOP_K, tm, d), F32), pltpu.SemaphoreType.DMA((2,))],
        compiler_params=_cparams(1, 48),
        name="combine",
    )(pos3, pos3, x, gates, g_final.reshape(1, d), ys)


def _moe_layer(x, g, w_router, wg, wu, wd, layer, g_final, *, tm, tf, t_prompt=None):
    t, d = x.shape
    n_exp = w_router.shape[1]
    h, meta, gates, cnt = _router(x, g, w_router, tm=tm)

    counts = cnt[0, :n_exp].astype(jnp.int32)
    padded = ((counts + tm - 1) // tm) * tm
    ends = jnp.cumsum(padded)
    base = ends - padded
    meta = meta.astype(jnp.int32)

    def positions(e, r):
        p = r
        for q in range(n_exp):
            p = p + jnp.where(e == q, base[q], 0)
        return p

    pos = jnp.stack([positions(meta[0], meta[2]), positions(meta[1], meta[3])], axis=0)
    pos3 = pos.reshape(TOP_K, t // tm, tm).transpose(1, 0, 2).reshape(t // tm, 1, TOP_K * tm)

    n_rows = TOP_K * t + n_exp * tm
    n_tiles = n_rows // tm
    tile_start = jnp.arange(n_tiles, dtype=jnp.int32) * tm
    tile_expert = jnp.minimum(
        jnp.sum((tile_start[:, None] >= ends[None, :]).astype(jnp.int32), axis=1), n_exp - 1)
    n_used = (ends[-1:] // tm).astype(jnp.int32)

    zero_start = (base + counts) // SUBLANES * SUBLANES
    xs = _dispatch(h, pos3, zero_start, n_rows, tm=tm)
    ys = _swiglu(xs, g, wg, wu, wd, layer, tile_expert, n_used, tm=tm, tf=tf, dense=False)
    return _combine(x, gates, pos3, ys, g_final, tm=tm, t_prompt=t_prompt)


def _fnet_fold_kernel(cs_ref, w_ref, o_ref):
    y = jnp.dot(cs_ref[...], w_ref[...], precision=lax.Precision.HIGHEST,
                preferred_element_type=F32)
    gd = w_ref.shape[0]
    o_ref[0, 0] = y[:gd].astype(BF)
    o_ref[1, 0] = y[gd:].astype(BF)


def _fnet_fold(w_out):
    d = w_out.shape[0]
    gd = d // FNET_GROUPS
    kk = (jnp.arange(gd, dtype=jnp.int32)[:, None] * jnp.arange(gd, dtype=jnp.int32)[None, :]) % gd
    ang = kk.astype(F32) * (2.0 * math.pi / gd)
    cs = jnp.concatenate([jnp.cos(ang), -jnp.sin(ang)], axis=0) * (gd ** -0.5)
    out = pl.pallas_call(
        _fnet_fold_kernel,
        grid=(FNET_GROUPS,),
        in_specs=[_resident((2 * gd, gd), lambda i: (0, 0)),
                  pl.BlockSpec((gd, d), lambda i: (i, 0))],
        out_specs=pl.BlockSpec((2, 1, gd, d), lambda i: (0, i, 0, 0)),
        out_shape=jax.ShapeDtypeStruct((2, FNET_GROUPS, gd, d), BF),
        compiler_params=_cparams(1, 32),
        name="fnet_fold",
    )(cs, w_out)
    return out.reshape(2 * d, d)


def _fnet1_kernel(x_ref, g_ref, m_ref, z_ref):
    n1, nsub, d = x_ref.shape
    rows = n1 * nsub
    h = _rms(x_ref[...].reshape(rows, d), g_ref[...]).astype(BF)
    for ri in range(2):
        z_ref[ri] = _dot(m_ref[0, ri * rows:(ri + 1) * rows, :], h).reshape(n1, nsub, d)


def _fnet2_kernel(z_ref, x_ref, m3_ref, wf_ref, o_ref, ab_ref):
    n2, kb, d = x_ref.shape
    m3 = m3_ref[...]
    for j in range(kb):
        zz = jnp.concatenate([z_ref[0, j], z_ref[1, j]], axis=0).astype(BF)
        ab = _dot(m3, zz)
        ab_ref[j * n2:(j + 1) * n2, :d] = ab[:n2].astype(BF)
        ab_ref[j * n2:(j + 1) * n2, d:] = ab[n2:].astype(BF)
    y = _dot(ab_ref[...], wf_ref[...])
    for j in range(kb):
        o_ref[:, j, :] = x_ref[:, j, :] + y[j * n2:(j + 1) * n2]


def _fnet_tables(seq, nsub):
    n2 = FNET_N2
    n1 = seq // n2
    steps = n2 // nsub
    k1 = jnp.arange(n1, dtype=jnp.int32)
    nn = n2 * jnp.arange(n1, dtype=jnp.int32)[:, None] + jnp.arange(n2, dtype=jnp.int32)[None, :]
    ang1 = ((k1[:, None, None] * nn[None]) % seq).astype(F32) * (2.0 * math.pi / seq)
    tab = jnp.stack([jnp.cos(ang1), jnp.sin(ang1)], axis=0) * (n1 ** -0.5)
    tab = tab.reshape(2, n1, n1, steps, nsub).transpose(3, 0, 1, 4, 2)
    eye = jnp.eye(nsub, dtype=F32)
    big = tab[..., None] * eye[None, None, None, :, None, :]
    m1 = big.reshape(steps, 2 * n1 * nsub, n1 * nsub).astype(BF)
    q = jnp.arange(n2, dtype=jnp.int32)
    ang3 = ((q[:, None] * q[None, :]) % n2).astype(F32) * (2.0 * math.pi / n2)
    c3, s3 = jnp.cos(ang3), jnp.sin(ang3)
    m3 = (jnp.concatenate([jnp.concatenate([c3, -s3], axis=1),
                           jnp.concatenate([s3, c3], axis=1)], axis=0) * (n2 ** -0.5)).astype(BF)
    return m1, m3


def _fnet_group(x, g, wfold, *, row0, bsz, seq):
    t, d = x.shape
    n2 = FNET_N2
    n1 = seq // n2
    nsub = min(FNET_ROWS // n1, n2)
    kb = min(SUBLANES, n1)
    m1, m3 = _fnet_tables(seq, nsub)
    rb = row0 // seq

    z = pl.pallas_call(
        _fnet1_kernel,
        grid=(bsz, n2 // nsub),
        in_specs=[
            pl.BlockSpec((n1, nsub, d), lambda b, j: (rb + b, j, 0)),
            _resident((1, d), lambda b, j: (0, 0)),
            pl.BlockSpec((1, 2 * n1 * nsub, n1 * nsub), lambda b, j: (j, 0, 0)),
        ],
        out_specs=pl.BlockSpec((None, 2, n1, nsub, d), lambda b, j: (b, 0, 0, j, 0)),
        out_shape=jax.ShapeDtypeStruct((bsz, 2, n1, n2, d), F32),
        compiler_params=_cparams(2, 56),
        name="fnet_stage1",
    )(x.reshape(t // n2, n2, d), g.reshape(1, d), m1)

    out = pl.pallas_call(
        _fnet2_kernel,
        grid=(bsz, n1 // kb),
        in_specs=[
            pl.BlockSpec((None, 2, kb, n2, d), lambda b, j: (b, 0, j, 0, 0)),
            pl.BlockSpec((n2, kb, d), lambda b, j: (rb + b, j, 0)),
            _resident((2 * n2, 2 * n2), lambda b, j: (0, 0)),
            _resident((2 * d, d), lambda b, j: (0, 0)),
        ],
        out_specs=pl.BlockSpec((n2, kb, d), lambda b, j: (rb + b, j, 0)),
        out_shape=jax.ShapeDtypeStruct((t // n1, n1, d), F32),
        scratch_shapes=[pltpu.VMEM((kb * n2, 2 * d), BF)],
        input_output_aliases={1: 0},
        compiler_params=_cparams(2, 56),
        name="fnet_stage2",
    )(z, x.reshape(t // n1, n1, d), m3, wfold)
    return out.reshape(t, d)


def _sgu_kernel(x_ref, g_ref, win_ref, vg_ref, ws_ref, bs_ref, wout_ref, o_ref, v_ref, sv_ref, *, tf):
    tm = x_ref.shape[0]
    half = vg_ref.shape[1]
    hd = half // GMLP_HEADS
    x = x_ref[...]
    h = _rms(x, g_ref[...]).astype(BF)
    sqrt_half = math.sqrt(0.5)

    def gelu(a):
        return 0.5 * a * (1.0 + lax.erf(a * sqrt_half))

    ssq = jnp.zeros((tm, 1), F32)
    for c in range(half // tf):
        vc = gelu(_dot(h, win_ref[:, half + c * tf:half + (c + 1) * tf]))
        sv_ref[:, c * tf:(c + 1) * tf] = vc
        ssq = ssq + jnp.sum(vc * vc, axis=-1, keepdims=True)
    inv = lax.rsqrt(ssq * (1.0 / half) + EPS)
    v_ref[...] = (sv_ref[...] * inv * vg_ref[...]).astype(BF)
    for r in range(tm // CHUNK):
        for hh in range(GMLP_HEADS):
            blk = _dot(ws_ref[hh], v_ref[r * CHUNK:(r + 1) * CHUNK, hh * hd:(hh + 1) * hd])
            sv_ref[r * CHUNK:(r + 1) * CHUNK, hh * hd:(hh + 1) * hd] = blk + bs_ref[hh][:, 0:1]
    acc = x
    for c in range(half // tf):
        uc = gelu(_dot(h, win_ref[:, c * tf:(c + 1) * tf]))
        acc = acc + _dot((uc * sv_ref[:, c * tf:(c + 1) * tf]).astype(BF),
                         wout_ref[c * tf:(c + 1) * tf, :])
    o_ref[...] = acc


def _sgu_mixer(x, g, w_in, v_gain, w_s, b_s, w_out, *, tm, tf):
    t, d = x.shape
    half = v_gain.shape[0]
    bs = jnp.broadcast_to(b_s[:, :, None], (GMLP_HEADS, CHUNK, LANES)).astype(F32)
    return pl.pallas_call(
        functools.partial(_sgu_kernel, tf=tf),
        grid=(t // tm,),
        in_specs=[
            pl.BlockSpec((tm, d), lambda i: (i, 0)),
            _resident((1, d), lambda i: (0, 0)),
            _resident((d, 2 * half), lambda i: (0, 0)),
            _resident((1, half), lambda i: (0, 0)),
            _resident((GMLP_HEADS, CHUNK, CHUNK), lambda i: (0, 0, 0)),
            _resident((GMLP_HEADS, CHUNK, LANES), lambda i: (0, 0, 0)),
            _resident((half, d), lambda i: (0, 0)),
        ],
        out_specs=pl.BlockSpec((tm, d), lambda i: (i, 0)),
        out_shape=jax.ShapeDtypeStruct((t, d), F32),
        scratch_shapes=[pltpu.VMEM((tm, half), BF), pltpu.VMEM((tm, half), F32)],
        compiler_params=_cparams(1, 56),
        name="sgu_mixer",
    )(x, g.reshape(1, d), w_in.astype(BF), v_gain.reshape(1, half), w_s.astype(BF), bs,
      w_out.astype(BF))


def _final_norm_kernel(x_ref, g_ref, o_ref):
    o_ref[...] = _rms(x_ref[...], g_ref[...])


def _final_norm(x, g, *, tm):
    t, d = x.shape
    return pl.pallas_call(
        _final_norm_kernel,
        grid=(t // tm,),
        in_specs=[pl.BlockSpec((tm, d), lambda i: (i, 0)), _resident((1, d), lambda i: (0, 0))],
        out_specs=pl.BlockSpec((tm, d), lambda i: (i, 0)),
        out_shape=jax.ShapeDtypeStruct((t, d), F32),
        compiler_params=_cparams(1, 32),
        name="final_norm",
    )(x, g.reshape(1, d))


def _tile(n, want):
    while n % want:
        want //= 2
    return want


def kernel(x_prompt, x_sample, norm_mix, norm_ffn, norm_final, conv_w_in, conv_w, conv_w_out, fnet_w_out, sgu_w_in, sgu_v_gain, sgu_w_s, sgu_b_s, sgu_w_out, ffn_w_gate, ffn_w_up, ffn_w_down, moe_w_router, moe_w_gate, moe_w_up, moe_w_down):
    bp, sp, d = x_prompt.shape
    bs_, ss, _ = x_sample.shape
    t_prompt = bp * sp
    t = t_prompt + bs_ * ss
    depth = norm_mix.shape[0]
    f = ffn_w_gate.shape[-1]
    tf = f // 7 if f % 7 == 0 else _tile(f, 512)
    tm = _tile(math.gcd(sp, ss), 512)
    half = sgu_v_gain.shape[-1]

    zero1 = jnp.zeros((t // tm,), jnp.int32)
    all_tiles = jnp.full((1,), t // tm, jnp.int32)
    ffn_w = [w.astype(BF)[:, None] for w in (ffn_w_gate, ffn_w_up, ffn_w_down)]
    moe_w = [w.astype(BF) for w in (moe_w_gate, moe_w_up, moe_w_down)]

    x = None
    for i in range(depth):
        m, j = i % N_MIXERS, i // N_MIXERS
        if m == 0:
            src = (x, None) if i else (x_prompt.reshape(t_prompt, d), x_sample.reshape(t - t_prompt, d))
            x = _conv_mixer(*src, norm_mix[i], conv_w_in[j], conv_w[j], conv_w_out[j], tm=tm,
                            t_prompt=t_prompt, s_prompt=sp, s_sample=ss)
        elif m == 1:
            wfold = _fnet_fold(fnet_w_out[j])
            x = _fnet_group(x, norm_mix[i], wfold, row0=0, bsz=bp, seq=sp)
            x = _fnet_group(x, norm_mix[i], wfold, row0=t_prompt, bsz=bs_, seq=ss)
        else:
            x = _sgu_mixer(x, norm_mix[i], sgu_w_in[j], sgu_v_gain[j], sgu_w_s[j], sgu_b_s[j],
                           sgu_w_out[j], tm=_tile(tm, 256), tf=_tile(half, 512))
        k = i // 2
        last = i == depth - 1
        if i % 2 == 0:
            x = _swiglu(x, norm_ffn[i], *ffn_w, k, zero1, all_tiles, tm=tm, tf=tf, dense=True)
            if last:
                x = _final_norm(x, norm_final, tm=tm)
                x = (x[:t_prompt], x[t_prompt:])
        else:
            x = _moe_layer(x, norm_ffn[i], moe_w_router[k], *moe_w, k, norm_final, tm=tm, tf=tf,
                           t_prompt=t_prompt if last else None)
    return (x[0].reshape(bp, sp, d), x[1].reshape(bs_, ss, d))
```

```python
import functools
import math

import jax
import jax.numpy as jnp
from jax import lax
from jax.experimental import pallas as pl
from jax.experimental.pallas import tpu as pltpu

EPS = 1e-6
BF = jnp.bfloat16
F32 = jnp.float32

N_MIXERS = 3
FNET_GROUPS = 8
GMLP_HEADS = 8
CHUNK = 128
TOP_K = 2
LANES = 128
SUBLANES = 8
HALO = 16
FNET_N2 = 128
FNET_ROWS = 1024


def _cparams(n_axes, vmem_mb):
    return pltpu.CompilerParams(
        dimension_semantics=("arbitrary",) * n_axes,
        vmem_limit_bytes=vmem_mb << 20,
    )


def _rms(x, g):
    ms = jnp.mean(x * x, axis=-1, keepdims=True)
    return x * lax.rsqrt(ms + EPS) * g


def _dot(a, b):
    return jnp.dot(a, b, preferred_element_type=F32)


def _resident(shape, index_map):
    return pl.BlockSpec(shape, index_map, pipeline_mode=pl.Buffered(1))


def _conv_kernel(*refs, tm, t_prompt, s_prompt, s_sample, two_sources):
    g_ref, win_ref, cw_ref, wout_ref, o_ref, h_ref, cz_ref = refs[-7:]
    i = pl.program_id(0)
    d = o_ref.shape[1]
    g = g_ref[...]
    if two_sources:
        in_prompt = i * tm < t_prompt
        xp, xc, xn = (jnp.where(in_prompt, a[...], b[...]) for a, b in zip(refs[0:3], refs[3:6]))
    else:
        xp, xc, xn = (a[...] for a in refs[0:3])

    def seq_boundary(r):
        return jnp.where(r <= t_prompt, r % s_prompt == 0, (r - t_prompt) % s_sample == 0)

    first = seq_boundary(i * tm)
    last = seq_boundary(i * tm + tm)
    h_ref[0:HALO, :] = jnp.where(first, 0.0, _rms(xp, g)).astype(BF)
    h_ref[HALO:HALO + tm, :] = _rms(xc, g).astype(BF)
    h_ref[HALO + tm:, :] = jnp.where(last, 0.0, _rms(xn, g)).astype(BF)

    hext = h_ref[...]
    c = _dot(hext, win_ref[:, d:2 * d])
    z = _dot(hext, win_ref[:, 2 * d:])
    cz_ref[...] = c * z
    b = _dot(h_ref[HALO:HALO + tm, :], win_ref[:, :d])
    cw = cw_ref[...]
    conv = (cz_ref[HALO - 1:HALO - 1 + tm, :] * cw[0:1, :]
            + cz_ref[HALO:HALO + tm, :] * cw[1:2, :]
            + cz_ref[HALO + 1:HALO + 1 + tm, :] * cw[2:3, :])
    y = _dot((b * conv).astype(BF), wout_ref[...])
    o_ref[...] = xc + y


def _conv_mixer(x, x_sample, g, w_in, conv_w, w_out, *, tm, t_prompt, s_prompt, s_sample):
    d = x.shape[1]
    xs = [x] if x_sample is None else [x, x_sample]
    t = sum(a.shape[0] for a in xs)
    nh = tm // HALO
    kern = functools.partial(_conv_kernel, tm=tm, t_prompt=t_prompt, s_prompt=s_prompt,
                             s_sample=s_sample, two_sources=len(xs) == 2)

    def source_specs(rows, tile0):
        nt, nhb = rows // tm, rows // HALO
        cur = lambda i: jnp.clip(i - tile0, 0, nt - 1)
        return [
            pl.BlockSpec((HALO, d), lambda i: (jnp.maximum(cur(i) * nh - 1, 0), 0)),
            pl.BlockSpec((tm, d), lambda i: (cur(i), 0)),
            pl.BlockSpec((HALO, d), lambda i: (jnp.minimum((cur(i) + 1) * nh, nhb - 1), 0)),
        ]

    if len(xs) == 2:
        in_specs = source_specs(t_prompt, 0) + source_specs(t - t_prompt, t_prompt // tm)
        args = [xs[0]] * 3 + [xs[1]] * 3
    else:
        in_specs = source_specs(t, 0)
        args = [xs[0]] * 3
    in_specs += [
        _resident((1, d), lambda i: (0, 0)),
        _resident((d, 3 * d), lambda i: (0, 0)),
        _resident((3, d), lambda i: (0, 0)),
        _resident((d, d), lambda i: (0, 0)),
    ]
    args += [g.reshape(1, d), w_in.astype(BF), conv_w, w_out.astype(BF)]
    return pl.pallas_call(
        kern,
        grid=(t // tm,),
        in_specs=in_specs,
        out_specs=pl.BlockSpec((tm, d), lambda i: (i, 0)),
        out_shape=jax.ShapeDtypeStruct((t, d), F32),
        scratch_shapes=[pltpu.VMEM((tm + 2 * HALO, d), BF),
                        pltpu.VMEM((tm + 2 * HALO, d), F32)],
        compiler_params=_cparams(1, 48),
        name="conv_mixer",
    )(*args)


def _swiglu_kernel(te_ref, nu_ref, x_ref, g_ref, wg_ref, wu_ref, wd_ref, o_ref, *, tf, dense):
    i = pl.program_id(0)

    @pl.when(i < nu_ref[0])
    def _():
        x = x_ref[...]
        h = (_rms(x, g_ref[...]) if dense else x).astype(BF)
        f = wg_ref.shape[-1]
        acc = x if dense else jnp.zeros_like(x)
        for c in range(f // tf):
            a = _dot(h, wg_ref[:, c * tf:(c + 1) * tf])
            b = _dot(h, wu_ref[:, c * tf:(c + 1) * tf])
            act = (jax.nn.silu(a) * b).astype(BF)
            acc = acc + _dot(act, wd_ref[c * tf:(c + 1) * tf, :])
        o_ref[...] = acc

    @pl.when(i >= nu_ref[0])
    def _():
        o_ref[...] = jnp.zeros_like(o_ref)


def _swiglu(x, g, wg, wu, wd, layer, tile_expert, n_used, *, tm, tf, dense):
    t, d = x.shape
    f = wg.shape[-1]
    kern = functools.partial(_swiglu_kernel, tf=tf, dense=dense)
    grid_spec = pltpu.PrefetchScalarGridSpec(
        num_scalar_prefetch=2,
        grid=(t // tm,),
        in_specs=[
            pl.BlockSpec((tm, d), lambda i, te, nu: (i, 0)),
            _resident((1, d), lambda i, te, nu: (0, 0)),
            _resident((None, None, d, f), lambda i, te, nu: (layer, te[i], 0, 0)),
            _resident((None, None, d, f), lambda i, te, nu: (layer, te[i], 0, 0)),
            _resident((None, None, f, d), lambda i, te, nu: (layer, te[i], 0, 0)),
        ],
        out_specs=pl.BlockSpec((tm, d), lambda i, te, nu: (i, 0)),
    )
    return pl.pallas_call(
        kern,
        grid_spec=grid_spec,
        out_shape=jax.ShapeDtypeStruct((t, d), F32),
        compiler_params=_cparams(1, 56),
        name="swiglu_dense" if dense else "swiglu_routed",
    )(tile_expert, n_used, x, g.reshape(1, d), wg, wu, wd)


def _router_kernel(x_ref, g_ref, wr_ref, tri_ref, h_ref, meta_ref, gate_ref, cnt_ref,
                   carry_ref, *, n_exp):
    i = pl.program_id(0)

    @pl.when(i == 0)
    def _():
        carry_ref[...] = jnp.zeros_like(carry_ref)

    h = _rms(x_ref[...], g_ref[...])
    h_ref[...] = h
    logits = jnp.dot(h, wr_ref[...], precision=lax.Precision.HIGHEST,
                     preferred_element_type=F32)
    lane = lax.broadcasted_iota(jnp.int32, logits.shape, 1).astype(F32)
    neg = jnp.float32(-jnp.inf)
    lg = jnp.where(lane < n_exp, logits, neg)
    v1 = jnp.max(lg, axis=1, keepdims=True)
    i1 = jnp.min(jnp.where(lg == v1, lane, float(LANES)), axis=1, keepdims=True)
    lg2 = jnp.where(lane == i1, neg, lg)
    v2 = jnp.max(lg2, axis=1, keepdims=True)
    i2 = jnp.min(jnp.where(lg2 == v2, lane, float(LANES)), axis=1, keepdims=True)
    e2 = jnp.exp(v2 - v1)
    g1 = 1.0 / (1.0 + e2)
    g2 = e2 / (1.0 + e2)

    sel = jnp.where((lane == i1) | (lane == i2), 1.0, 0.0)
    cum = _dot(tri_ref[...], sel.astype(BF)) + carry_ref[...]
    r1 = jnp.sum(jnp.where(lane == i1, cum, 0.0), axis=1, keepdims=True)
    r2 = jnp.sum(jnp.where(lane == i2, cum, 0.0), axis=1, keepdims=True)
    carry_ref[...] = carry_ref[...] + jnp.sum(sel, axis=0, keepdims=True)
    cnt_ref[...] = carry_ref[...]

    meta = jnp.where(lane == 0, i1, jnp.where(lane == 1, i2, jnp.where(
        lane == 2, r1, jnp.where(lane == 3, r2, 0.0))))
    meta_ref[...] = meta.T[0:SUBLANES, :]
    gate_ref[...] = jnp.where(lane == 0, g1, jnp.where(lane == 1, g2, 0.0))


def _router(x, g, w_router, *, tm):
    t, d = x.shape
    n_exp = w_router.shape[1]
    wr = jnp.zeros((d, LANES), F32).at[:, :n_exp].set(w_router.astype(F32))
    row = lax.broadcasted_iota(jnp.int32, (tm, tm), 0)
    col = lax.broadcasted_iota(jnp.int32, (tm, tm), 1)
    tri = (col < row).astype(BF)
    kern = functools.partial(_router_kernel, n_exp=n_exp)
    return pl.pallas_call(
        kern,
        grid=(t // tm,),
        in_specs=[
            pl.BlockSpec((tm, d), lambda i: (i, 0)),
            _resident((1, d), lambda i: (0, 0)),
            _resident((d, LANES), lambda i: (0, 0)),
            _resident((tm, tm), lambda i: (0, 0)),
        ],
        out_specs=[
            pl.BlockSpec((tm, d), lambda i: (i, 0)),
            pl.BlockSpec((SUBLANES, tm), lambda i: (0, i)),
            pl.BlockSpec((tm, LANES), lambda i: (i, 0)),
            pl.BlockSpec((1, LANES), lambda i: (0, 0)),
        ],
        out_shape=[
            jax.ShapeDtypeStruct((t, d), F32),
            jax.ShapeDtypeStruct((SUBLANES, t), F32),
            jax.ShapeDtypeStruct((t, LANES), F32),
            jax.ShapeDtypeStruct((1, LANES), F32),
        ],
        scratch_shapes=[pltpu.VMEM((1, LANES), F32)],
        compiler_params=_cparams(1, 32),
        name="router",
    )(x, g.reshape(1, d), wr, tri)


def _dispatch_kernel(zs_ref, pos_ref, h_ref, xs_ref, zero_ref, sem, zsem, *, n_exp):
    tm = h_ref.shape[0]

    @pl.when(pl.program_id(0) == 0)
    def _():
        zero_ref[...] = jnp.zeros_like(zero_ref)
        nz = zero_ref.shape[0]
        n_rows = xs_ref.shape[0]
        for e in range(n_exp):
            start = pl.multiple_of(zs_ref[e], SUBLANES)
            cp = pltpu.make_async_copy(zero_ref, xs_ref.at[pl.ds(start, nz), :], zsem)
            cp.start()
            cp.wait()
            cp = pltpu.make_async_copy(zero_ref.at[pl.ds(0, tm), :],
                                       xs_ref.at[pl.ds(n_rows - (e + 1) * tm, tm), :], zsem)
            cp.start()
            cp.wait()

    for r in range(tm):
        for k in range(TOP_K):
            p = pos_ref[0, 0, k * tm + r]
            pltpu.make_async_copy(h_ref.at[pl.ds(r, 1), :], xs_ref.at[pl.ds(p, 1), :],
                                  sem).start(priority=k)
    for k in range(TOP_K):
        pltpu.make_async_copy(h_ref, xs_ref.at[pl.ds(0, tm), :], sem).wait()


def _dispatch(h, pos3, zero_start, n_rows, *, tm):
    t, d = h.shape
    n_exp = zero_start.shape[0]
    grid_spec = pltpu.PrefetchScalarGridSpec(
        num_scalar_prefetch=1,
        grid=(t // tm,),
        in_specs=[
            pl.BlockSpec((1, 1, TOP_K * tm), lambda i, zs: (i, 0, 0), memory_space=pltpu.SMEM),
            pl.BlockSpec((tm, d), lambda i, zs: (i, 0)),
        ],
        out_specs=pl.BlockSpec(memory_space=pl.ANY),
        scratch_shapes=[pltpu.VMEM((tm + SUBLANES, d), F32), pltpu.SemaphoreType.DMA(()),
                        pltpu.SemaphoreType.DMA(())],
    )
    return pl.pallas_call(
        functools.partial(_dispatch_kernel, n_exp=n_exp),
        grid_spec=grid_spec,
        out_shape=jax.ShapeDtypeStruct((n_rows, d), F32),
        compiler_params=_cparams(1, 32),
        name="dispatch",
    )(zero_start, pos3, h)


def _gather_rows(pos_ref, ys_ref, buf_ref, sem, slot, tm):
    for r in range(tm):
        for k in range(TOP_K):
            p = pos_ref[0, 0, k * tm + r]
            pltpu.make_async_copy(ys_ref.at[pl.ds(p, 1), :], buf_ref.at[slot, k, pl.ds(r, 1), :],
                                  sem.at[slot]).start(priority=k)


def _combine_kernel(pos_ref, posn_ref, x_ref, gate_ref, gf_ref, ys_ref, *rest, n_prompt_tiles):
    if n_prompt_tiles is None:
        (o_ref,), (buf_ref, sem) = rest[:1], rest[1:]
    else:
        (op_ref, os_ref), (buf_ref, sem) = rest[:2], rest[2:]
    i = pl.program_id(0)
    n = pl.num_programs(0)
    tm = x_ref.shape[0]
    slot = i % 2

    @pl.when(i == 0)
    def _():
        _gather_rows(pos_ref, ys_ref, buf_ref, sem, 0, tm)

    @pl.when(i + 1 < n)
    def _():
        _gather_rows(posn_ref, ys_ref, buf_ref, sem, 1 - slot, tm)

    for k in range(TOP_K):
        pltpu.make_async_copy(ys_ref.at[pl.ds(0, tm), :], buf_ref.at[slot, k], sem.at[slot]).wait()
    gate = gate_ref[...]
    y = x_ref[...] + gate[:, 0:1] * buf_ref[slot, 0] + gate[:, 1:2] * buf_ref[slot, 1]
    if n_prompt_tiles is None:
        o_ref[...] = y
    else:
        y = _rms(y, gf_ref[...])

        @pl.when(i < n_prompt_tiles)
        def _():
            op_ref[...] = y

        @pl.when(i >= n_prompt_tiles)
        def _():
            os_ref[...] = y


def _combine(x, gates, pos3, ys, g_final, *, tm, t_prompt=None):
    t, d = x.shape
    nt = t // tm
    if t_prompt is None:
        npt = None
        out_specs = pl.BlockSpec((tm, d), lambda i: (i, 0))
        out_shape = jax.ShapeDtypeStruct((t, d), F32)
    else:
        npt = t_prompt // tm
        out_specs = [pl.BlockSpec((tm, d), lambda i: (jnp.minimum(i, npt - 1), 0)),
                     pl.BlockSpec((tm, d), lambda i: (jnp.maximum(i - npt, 0), 0))]
        out_shape = [jax.ShapeDtypeStruct((t_prompt, d), F32),
                     jax.ShapeDtypeStruct((t - t_prompt, d), F32)]
    return pl.pallas_call(
        functools.partial(_combine_kernel, n_prompt_tiles=npt),
        grid=(nt,),
        in_specs=[
            pl.BlockSpec((1, 1, TOP_K * tm), lambda i: (i, 0, 0), memory_space=pltpu.SMEM),
            pl.BlockSpec((1, 1, TOP_K * tm), lambda i: (jnp.minimum(i + 1, nt - 1), 0, 0),
                         memory_space=pltpu.SMEM),
            pl.BlockSpec((tm, d), lambda i: (i, 0)),
            pl.BlockSpec((tm, LANES), lambda i: (i, 0)),
            _resident((1, d), lambda i: (0, 0)),
            pl.BlockSpec(memory_space=pl.ANY),
        ],
        out_specs=out_specs,
        out_shape=out_shape,
        scratch_shapes=[pltpu.VMEM((2, TOP_K, tm, d), F32), pltpu.SemaphoreType.DMA((2,))],
        compiler_params=_cparams(1, 48),
        name="combine",
    )(pos3, pos3, x, gates, g_final.reshape(1, d), ys)


def _moe_layer(x, g, w_router, wg, wu, wd, layer, g_final, *, tm, tf, t_prompt=None):
    t, d = x.shape
    n_exp = w_router.shape[1]
    h, meta, gates, cnt = _router(x, g, w_router, tm=tm)

    counts = cnt[0, :n_exp].astype(jnp.int32)
    padded = ((counts + tm - 1) // tm) * tm
    ends = jnp.cumsum(padded)
    base = ends - padded
    meta = meta.astype(jnp.int32)

    def positions(e, r):
        p = r
        for q in range(n_exp):
            p = p + jnp.where(e == q, base[q], 0)
        return p

    pos = jnp.stack([positions(meta[0], meta[2]), positions(meta[1], meta[3])], axis=0)
    pos3 = pos.reshape(TOP_K, t // tm, tm).transpose(1, 0, 2).reshape(t // tm, 1, TOP_K * tm)

    n_rows = TOP_K * t + n_exp * tm
    n_tiles = n_rows // tm
    tile_start = jnp.arange(n_tiles, dtype=jnp.int32) * tm
    tile_expert = jnp.minimum(
        jnp.sum((tile_start[:, None] >= ends[None, :]).astype(jnp.int32), axis=1), n_exp - 1)
    n_used = (ends[-1:] // tm).astype(jnp.int32)

    zero_start = (base + counts) // SUBLANES * SUBLANES
    xs = _dispatch(h, pos3, zero_start, n_rows, tm=tm)
    ys = _swiglu(xs, g, wg, wu, wd, layer, tile_expert, n_used, tm=tm, tf=tf, dense=False)
    return _combine(x, gates, pos3, ys, g_final, tm=tm, t_prompt=t_prompt)


def _fnet_fold_kernel(cs_ref, w_ref, o_ref):
    y = jnp.dot(cs_ref[...], w_ref[...], precision=lax.Precision.HIGHEST,
                preferred_element_type=F32)
    gd = w_ref.shape[0]
    o_ref[0, 0] = y[:gd].astype(BF)
    o_ref[1, 0] = y[gd:].astype(BF)


def _fnet_fold(w_out):
    d = w_out.shape[0]
    gd = d // FNET_GROUPS
    kk = (jnp.arange(gd, dtype=jnp.int32)[:, None] * jnp.arange(gd, dtype=jnp.int32)[None, :]) % gd
    ang = kk.astype(F32) * (2.0 * math.pi / gd)
    cs = jnp.concatenate([jnp.cos(ang), -jnp.sin(ang)], axis=0) * (gd ** -0.5)
    out = pl.pallas_call(
        _fnet_fold_kernel,
        grid=(FNET_GROUPS,),
        in_specs=[_resident((2 * gd, gd), lambda i: (0, 0)),
                  pl.BlockSpec((gd, d), lambda i: (i, 0))],
        out_specs=pl.BlockSpec((2, 1, gd, d), lambda i: (0, i, 0, 0)),
        out_shape=jax.ShapeDtypeStruct((2, FNET_GROUPS, gd, d), BF),
        compiler_params=_cparams(1, 32),
        name="fnet_fold",
    )(cs, w_out)
    return out.reshape(2 * d, d)


def _fnet1_kernel(x_ref, g_ref, t_ref, rep_ref, z_ref):
    n1, nsub, d = x_ref.shape
    rows = n1 * nsub
    h = _rms(x_ref[...].reshape(rows, d), g_ref[...]).astype(BF)
    r = lax.broadcasted_iota(jnp.int32, (rows, rows), 0)
    c = lax.broadcasted_iota(jnp.int32, (rows, rows), 1)
    same_n2 = (r % nsub) == (c % nsub)
    for ri in range(2):
        rep = _dot(rep_ref[...], t_ref[0, ri * n1:(ri + 1) * n1, :])
        m = jnp.where(same_n2, rep, 0.0).astype(BF)
        z_ref[ri] = _dot(m, h).reshape(n1, nsub, d)


def _fnet2_kernel(z_ref, x_ref, m3_ref, wf_ref, o_ref, ab_ref):
    n2, kb, d = x_ref.shape
    m3 = m3_ref[...]
    for j in range(kb):
        zz = jnp.concatenate([z_ref[0, j], z_ref[1, j]], axis=0).astype(BF)
        ab = _dot(m3, zz)
        ab_ref[j * n2:(j + 1) * n2, :d] = ab[:n2].astype(BF)
        ab_ref[j * n2:(j + 1) * n2, d:] = ab[n2:].astype(BF)
    y = _dot(ab_ref[...], wf_ref[...])
    for j in range(kb):
        o_ref[:, j, :] = x_ref[:, j, :] + y[j * n2:(j + 1) * n2]


def _fnet_tables(seq, nsub):
    n2 = FNET_N2
    n1 = seq // n2
    steps = n2 // nsub
    k1 = jnp.arange(n1, dtype=jnp.int32)
    nn = n2 * jnp.arange(n1, dtype=jnp.int32)[:, None] + jnp.arange(n2, dtype=jnp.int32)[None, :]
    ang1 = ((k1[:, None, None] * nn[None]) % seq).astype(F32) * (2.0 * math.pi / seq)
    tab = jnp.stack([jnp.cos(ang1), jnp.sin(ang1)], axis=0) * (n1 ** -0.5)
    m1 = tab.reshape(2, n1, n1, steps, nsub).transpose(3, 0, 1, 2, 4).reshape(
        steps, 2 * n1, n1 * nsub).astype(BF)
    rep = (jnp.arange(n1 * nsub, dtype=jnp.int32)[:, None] // nsub
           == jnp.arange(n1, dtype=jnp.int32)[None, :]).astype(BF)
    q = jnp.arange(n2, dtype=jnp.int32)
    ang3 = ((q[:, None] * q[None, :]) % n2).astype(F32) * (2.0 * math.pi / n2)
    c3, s3 = jnp.cos(ang3), jnp.sin(ang3)
    m3 = (jnp.concatenate([jnp.concatenate([c3, -s3], axis=1),
                           jnp.concatenate([s3, c3], axis=1)], axis=0) * (n2 ** -0.5)).astype(BF)
    return m1, rep, m3


def _fnet_group(x, g, wfold, *, row0, bsz, seq):
    t, d = x.shape
    n2 = FNET_N2
    n1 = seq // n2
    nsub = min(FNET_ROWS // n1, n2)
    kb = min(SUBLANES, n1)
    m1, rep, m3 = _fnet_tables(seq, nsub)
    rb = row0 // seq

    z = pl.pallas_call(
        _fnet1_kernel,
        grid=(bsz, n2 // nsub),
        in_specs=[
            pl.BlockSpec((n1, nsub, d), lambda b, j: (rb + b, j, 0)),
            _resident((1, d), lambda b, j: (0, 0)),
            pl.BlockSpec((1, 2 * n1, n1 * nsub), lambda b, j: (j, 0, 0)),
            _resident((n1 * nsub, n1), lambda b, j: (0, 0)),
        ],
        out_specs=pl.BlockSpec((None, 2, n1, nsub, d), lambda b, j: (b, 0, 0, j, 0)),
        out_shape=jax.ShapeDtypeStruct((bsz, 2, n1, n2, d), F32),
        compiler_params=_cparams(2, 56),
        name="fnet_stage1",
    )(x.reshape(t // n2, n2, d), g.reshape(1, d), m1, rep)

    out = pl.pallas_call(
        _fnet2_kernel,
        grid=(bsz, n1 // kb),
        in_specs=[
            pl.BlockSpec((None, 2, kb, n2, d), lambda b, j: (b, 0, j, 0, 0)),
            pl.BlockSpec((n2, kb, d), lambda b, j: (rb + b, j, 0)),
            _resident((2 * n2, 2 * n2), lambda b, j: (0, 0)),
            _resident((2 * d, d), lambda b, j: (0, 0)),
        ],
        out_specs=pl.BlockSpec((n2, kb, d), lambda b, j: (rb + b, j, 0)),
        out_shape=jax.ShapeDtypeStruct((t // n1, n1, d), F32),
        scratch_shapes=[pltpu.VMEM((kb * n2, 2 * d), BF)],
        input_output_aliases={1: 0},
        compiler_params=_cparams(2, 56),
        name="fnet_stage2",
    )(z, x.reshape(t // n1, n1, d), m3, wfold)
    return out.reshape(t, d)


def _sgu_kernel(x_ref, g_ref, win_ref, vg_ref, ws_ref, bs_ref, wout_ref, o_ref, v_ref, sv_ref, *, tf):
    tm = x_ref.shape[0]
    half = vg_ref.shape[1]
    hd = half // GMLP_HEADS
    x = x_ref[...]
    h = _rms(x, g_ref[...]).astype(BF)
    sqrt_half = math.sqrt(0.5)

    def gelu(a):
        return 0.5 * a * (1.0 + lax.erf(a * sqrt_half))

    ssq = jnp.zeros((tm, 1), F32)
    for c in range(half // tf):
        vc = gelu(_dot(h, win_ref[:, half + c * tf:half + (c + 1) * tf]))
        sv_ref[:, c * tf:(c + 1) * tf] = vc
        ssq = ssq + jnp.sum(vc * vc, axis=-1, keepdims=True)
    inv = lax.rsqrt(ssq * (1.0 / half) + EPS)
    v_ref[...] = (sv_ref[...] * inv * vg_ref[...]).astype(BF)
    nr = tm // CHUNK
    for hh in range(GMLP_HEADS):
        vv = jnp.concatenate([v_ref[r * CHUNK:(r + 1) * CHUNK, hh * hd:(hh + 1) * hd]
                              for r in range(nr)], axis=1)
        blk = _dot(ws_ref[hh], vv) + bs_ref[hh][:, 0:1]
        for r in range(nr):
            sv_ref[r * CHUNK:(r + 1) * CHUNK, hh * hd:(hh + 1) * hd] = blk[:, r * hd:(r + 1) * hd]
    acc = x
    for c in range(half // tf):
        uc = gelu(_dot(h, win_ref[:, c * tf:(c + 1) * tf]))
        acc = acc + _dot((uc * sv_ref[:, c * tf:(c + 1) * tf]).astype(BF),
                         wout_ref[c * tf:(c + 1) * tf, :])
    o_ref[...] = acc


def _sgu_mixer(x, g, w_in, v_gain, w_s, b_s, w_out, *, tm, tf):
    t, d = x.shape
    half = v_gain.shape[0]
    bs = jnp.broadcast_to(b_s[:, :, None], (GMLP_HEADS, CHUNK, LANES)).astype(F32)
    return pl.pallas_call(
        functools.partial(_sgu_kernel, tf=tf),
        grid=(t // tm,),
        in_specs=[
            pl.BlockSpec((tm, d), lambda i: (i, 0)),
            _resident((1, d), lambda i: (0, 0)),
            _resident((d, 2 * half), lambda i: (0, 0)),
            _resident((1, half), lambda i: (0, 0)),
            _resident((GMLP_HEADS, CHUNK, CHUNK), lambda i: (0, 0, 0)),
            _resident((GMLP_HEADS, CHUNK, LANES), lambda i: (0, 0, 0)),
            _resident((half, d), lambda i: (0, 0)),
        ],
        out_specs=pl.BlockSpec((tm, d), lambda i: (i, 0)),
        out_shape=jax.ShapeDtypeStruct((t, d), F32),
        scratch_shapes=[pltpu.VMEM((tm, half), BF), pltpu.VMEM((tm, half), F32)],
        compiler_params=_cparams(1, 56),
        name="sgu_mixer",
    )(x, g.reshape(1, d), w_in.astype(BF), v_gain.reshape(1, half), w_s.astype(BF), bs,
      w_out.astype(BF))


def _final_norm_kernel(x_ref, g_ref, o_ref):
    o_ref[...] = _rms(x_ref[...], g_ref[...])


def _final_norm(x, g, *, tm):
    t, d = x.shape
    return pl.pallas_call(
        _final_norm_kernel,
        grid=(t // tm,),
        in_specs=[pl.BlockSpec((tm, d), lambda i: (i, 0)), _resident((1, d), lambda i: (0, 0))],
        out_specs=pl.BlockSpec((tm, d), lambda i: (i, 0)),
        out_shape=jax.ShapeDtypeStruct((t, d), F32),
        compiler_params=_cparams(1, 32),
        name="final_norm",
    )(x, g.reshape(1, d))


def _tile(n, want):
    while n % want:
        want //= 2
    return want


def kernel(x_prompt, x_sample, norm_mix, norm_ffn, norm_final, conv_w_in, conv_w, conv_w_out, fnet_w_out, sgu_w_in, sgu_v_gain, sgu_w_s, sgu_b_s, sgu_w_out, ffn_w_gate, ffn_w_up, ffn_w_down, moe_w_router, moe_w_gate, moe_w_up, moe_w_down):
    bp, sp, d = x_prompt.shape
    bs_, ss, _ = x_sample.shape
    t_prompt = bp * sp
    t = t_prompt + bs_ * ss
    depth = norm_mix.shape[0]
    f = ffn_w_gate.shape[-1]
    tf = f // 7 if f % 7 == 0 else _tile(f, 512)
    tm = _tile(math.gcd(sp, ss), 512)
    half = sgu_v_gain.shape[-1]

    zero1 = jnp.zeros((t // tm,), jnp.int32)
    all_tiles = jnp.full((1,), t // tm, jnp.int32)
    ffn_w = [w.astype(BF)[:, None] for w in (ffn_w_gate, ffn_w_up, ffn_w_down)]
    moe_w = [w.astype(BF) for w in (moe_w_gate, moe_w_up, moe_w_down)]

    x = None
    for i in range(depth):
        m, j = i % N_MIXERS, i // N_MIXERS
        if m == 0:
            src = (x, None) if i else (x_prompt.reshape(t_prompt, d), x_sample.reshape(t - t_prompt, d))
            x = _conv_mixer(*src, norm_mix[i], conv_w_in[j], conv_w[j], conv_w_out[j], tm=tm,
                            t_prompt=t_prompt, s_prompt=sp, s_sample=ss)
        elif m == 1:
            wfold = _fnet_fold(fnet_w_out[j])
            x = _fnet_group(x, norm_mix[i], wfold, row0=0, bsz=bp, seq=sp)
            x = _fnet_group(x, norm_mix[i], wfold, row0=t_prompt, bsz=bs_, seq=ss)
        else:
            x = _sgu_mixer(x, norm_mix[i], sgu_w_in[j], sgu_v_gain[j], sgu_w_s[j], sgu_b_s[j],
                           sgu_w_out[j], tm=tm, tf=_tile(half, 512))
        k = i // 2
        last = i == depth - 1
        if i % 2 == 0:
            x = _swiglu(x, norm_ffn[i], *ffn_w, k, zero1, all_tiles, tm=tm, tf=tf, dense=True)
            if last:
                x = _final_norm(x, norm_final, tm=tm)
                x = (x[:t_prompt], x[t_prompt:])
        else:
            x = _moe_layer(x, norm_ffn[i], moe_w_router[k], *moe_w, k, norm_final, tm=tm, tf=tf,
                           t_prompt=t_prompt if last else None)
    return (x[0].reshape(bp, sp, d), x[1].reshape(bs_, ss, d))
```

```python
import functools
import math

import jax
import jax.numpy as jnp
from jax import lax
from jax.experimental import pallas as pl
from jax.experimental.pallas import tpu as pltpu

EPS = 1e-6
BF = jnp.bfloat16
F32 = jnp.float32

N_MIXERS = 3
FNET_GROUPS = 8
GMLP_HEADS = 8
CHUNK = 128
TOP_K = 2
LANES = 128
SUBLANES = 8
HALO = 16
FNET_N2 = 128
FNET_ROWS = 1024


def _cparams(n_axes, vmem_mb):
    return pltpu.CompilerParams(
        dimension_semantics=("arbitrary",) * n_axes,
        vmem_limit_bytes=vmem_mb << 20,
    )


def _rms(x, g):
    ms = jnp.mean(x * x, axis=-1, keepdims=True)
    return x * lax.rsqrt(ms + EPS) * g


def _dot(a, b):
    return jnp.dot(a, b, preferred_element_type=F32)


def _resident(shape, index_map):
    return pl.BlockSpec(shape, index_map, pipeline_mode=pl.Buffered(1))


def _conv_kernel(*refs, tm, t_prompt, s_prompt, s_sample, two_sources):
    g_ref, win_ref, cw_ref, wout_ref, o_ref, h_ref, cz_ref = refs[-7:]
    i = pl.program_id(0)
    d = o_ref.shape[1]
    g = g_ref[...]
    if two_sources:
        in_prompt = i * tm < t_prompt
        xp, xc, xn = (jnp.where(in_prompt, a[...], b[...]) for a, b in zip(refs[0:3], refs[3:6]))
    else:
        xp, xc, xn = (a[...] for a in refs[0:3])

    def seq_boundary(r):
        return jnp.where(r <= t_prompt, r % s_prompt == 0, (r - t_prompt) % s_sample == 0)

    first = seq_boundary(i * tm)
    last = seq_boundary(i * tm + tm)
    h_ref[0:HALO, :] = jnp.where(first, 0.0, _rms(xp, g)).astype(BF)
    h_ref[HALO:HALO + tm, :] = _rms(xc, g).astype(BF)
    h_ref[HALO + tm:, :] = jnp.where(last, 0.0, _rms(xn, g)).astype(BF)

    hext = h_ref[...]
    c = _dot(hext, win_ref[:, d:2 * d])
    z = _dot(hext, win_ref[:, 2 * d:])
    cz_ref[...] = c * z
    b = _dot(h_ref[HALO:HALO + tm, :], win_ref[:, :d])
    cw = cw_ref[...]
    conv = (cz_ref[HALO - 1:HALO - 1 + tm, :] * cw[0:1, :]
            + cz_ref[HALO:HALO + tm, :] * cw[1:2, :]
            + cz_ref[HALO + 1:HALO + 1 + tm, :] * cw[2:3, :])
    y = _dot((b * conv).astype(BF), wout_ref[...])
    o_ref[...] = xc + y


def _conv_mixer(x, x_sample, g, w_in, conv_w, w_out, *, tm, t_prompt, s_prompt, s_sample):
    d = x.shape[1]
    xs = [x] if x_sample is None else [x, x_sample]
    t = sum(a.shape[0] for a in xs)
    nh = tm // HALO
    kern = functools.partial(_conv_kernel, tm=tm, t_prompt=t_prompt, s_prompt=s_prompt,
                             s_sample=s_sample, two_sources=len(xs) == 2)

    def source_specs(rows, tile0):
        nt, nhb = rows // tm, rows // HALO
        cur = lambda i: jnp.clip(i - tile0, 0, nt - 1)
        return [
            pl.BlockSpec((HALO, d), lambda i: (jnp.maximum(cur(i) * nh - 1, 0), 0)),
            pl.BlockSpec((tm, d), lambda i: (cur(i), 0)),
            pl.BlockSpec((HALO, d), lambda i: (jnp.minimum((cur(i) + 1) * nh, nhb - 1), 0)),
        ]

    if len(xs) == 2:
        in_specs = source_specs(t_prompt, 0) + source_specs(t - t_prompt, t_prompt // tm)
        args = [xs[0]] * 3 + [xs[1]] * 3
    else:
        in_specs = source_specs(t, 0)
        args = [xs[0]] * 3
    in_specs += [
        _resident((1, d), lambda i: (0, 0)),
        _resident((d, 3 * d), lambda i: (0, 0)),
        _resident((3, d), lambda i: (0, 0)),
        _resident((d, d), lambda i: (0, 0)),
    ]
    args += [g.reshape(1, d), w_in.astype(BF), conv_w, w_out.astype(BF)]
    return pl.pallas_call(
        kern,
        grid=(t // tm,),
        in_specs=in_specs,
        out_specs=pl.BlockSpec((tm, d), lambda i: (i, 0)),
        out_shape=jax.ShapeDtypeStruct((t, d), F32),
        scratch_shapes=[pltpu.VMEM((tm + 2 * HALO, d), BF),
                        pltpu.VMEM((tm + 2 * HALO, d), F32)],
        compiler_params=_cparams(1, 48),
        name="conv_mixer",
    )(*args)


def _swiglu_kernel(te_ref, nu_ref, x_ref, g_ref, wg_ref, wu_ref, wd_ref, o_ref, *, tf, dense):
    i = pl.program_id(0)

    @pl.when(i < nu_ref[0])
    def _():
        x = x_ref[...]
        h = (_rms(x, g_ref[...]) if dense else x).astype(BF)
        f = wg_ref.shape[-1]
        acc = x if dense else jnp.zeros_like(x)
        for c in range(f // tf):
            a = _dot(h, wg_ref[:, c * tf:(c + 1) * tf])
            b = _dot(h, wu_ref[:, c * tf:(c + 1) * tf])
            act = (jax.nn.silu(a) * b).astype(BF)
            acc = acc + _dot(act, wd_ref[c * tf:(c + 1) * tf, :])
        o_ref[...] = acc

    @pl.when(i >= nu_ref[0])
    def _():
        o_ref[...] = jnp.zeros_like(o_ref)


def _swiglu(x, g, wg, wu, wd, layer, tile_expert, n_used, *, tm, tf, dense):
    t, d = x.shape
    f = wg.shape[-1]
    kern = functools.partial(_swiglu_kernel, tf=tf, dense=dense)
    grid_spec = pltpu.PrefetchScalarGridSpec(
        num_scalar_prefetch=2,
        grid=(t // tm,),
        in_specs=[
            pl.BlockSpec((tm, d), lambda i, te, nu: (i, 0)),
            _resident((1, d), lambda i, te, nu: (0, 0)),
            _resident((None, None, d, f), lambda i, te, nu: (layer, te[i], 0, 0)),
            _resident((None, None, d, f), lambda i, te, nu: (layer, te[i], 0, 0)),
            _resident((None, None, f, d), lambda i, te, nu: (layer, te[i], 0, 0)),
        ],
        out_specs=pl.BlockSpec((tm, d), lambda i, te, nu: (i, 0)),
    )
    return pl.pallas_call(
        kern,
        grid_spec=grid_spec,
        out_shape=jax.ShapeDtypeStruct((t, d), F32),
        compiler_params=_cparams(1, 56),
        name="swiglu_dense" if dense else "swiglu_routed",
    )(tile_expert, n_used, x, g.reshape(1, d), wg, wu, wd)


def _router_kernel(x_ref, g_ref, wr_ref, tri_ref, meta_ref, gate_ref, cnt_ref,
                   carry_ref, *, n_exp):
    i = pl.program_id(0)

    @pl.when(i == 0)
    def _():
        carry_ref[...] = jnp.zeros_like(carry_ref)

    h = _rms(x_ref[...], g_ref[...])
    h_hi = h.astype(BF)
    h_lo = (h - h_hi.astype(F32)).astype(BF)
    hw = _dot(h_hi, wr_ref[...])
    logits = hw[:, :LANES] + hw[:, LANES:] + _dot(h_lo, wr_ref[:, :LANES])
    lane = lax.broadcasted_iota(jnp.int32, logits.shape, 1).astype(F32)
    neg = jnp.float32(-jnp.inf)
    lg = jnp.where(lane < n_exp, logits, neg)
    v1 = jnp.max(lg, axis=1, keepdims=True)
    i1 = jnp.min(jnp.where(lg == v1, lane, float(LANES)), axis=1, keepdims=True)
    lg2 = jnp.where(lane == i1, neg, lg)
    v2 = jnp.max(lg2, axis=1, keepdims=True)
    i2 = jnp.min(jnp.where(lg2 == v2, lane, float(LANES)), axis=1, keepdims=True)
    e2 = jnp.exp(v2 - v1)
    g1 = 1.0 / (1.0 + e2)
    g2 = e2 / (1.0 + e2)

    sel = jnp.where((lane == i1) | (lane == i2), 1.0, 0.0)
    cum = _dot(tri_ref[...], sel.astype(BF)) + carry_ref[...]
    r1 = jnp.sum(jnp.where(lane == i1, cum, 0.0), axis=1, keepdims=True)
    r2 = jnp.sum(jnp.where(lane == i2, cum, 0.0), axis=1, keepdims=True)
    carry_ref[...] = carry_ref[...] + jnp.sum(sel, axis=0, keepdims=True)
    cnt_ref[...] = carry_ref[...]

    meta = jnp.where(lane == 0, i1, jnp.where(lane == 1, i2, jnp.where(
        lane == 2, r1, jnp.where(lane == 3, r2, 0.0))))
    meta_ref[...] = meta.T[0:SUBLANES, :]
    gate_ref[...] = jnp.where(lane == 0, g1, jnp.where(lane == 1, g2, 0.0))


def _router(x, g, w_router, *, tm):
    t, d = x.shape
    n_exp = w_router.shape[1]
    wr = jnp.zeros((d, LANES), F32).at[:, :n_exp].set(w_router.astype(F32))
    wr_hi = wr.astype(BF)
    wr = jnp.concatenate([wr_hi, (wr - wr_hi.astype(F32)).astype(BF)], axis=1)
    row = lax.broadcasted_iota(jnp.int32, (tm, tm), 0)
    col = lax.broadcasted_iota(jnp.int32, (tm, tm), 1)
    tri = (col < row).astype(BF)
    kern = functools.partial(_router_kernel, n_exp=n_exp)
    return pl.pallas_call(
        kern,
        grid=(t // tm,),
        in_specs=[
            pl.BlockSpec((tm, d), lambda i: (i, 0)),
            _resident((1, d), lambda i: (0, 0)),
            _resident((d, 2 * LANES), lambda i: (0, 0)),
            _resident((tm, tm), lambda i: (0, 0)),
        ],
        out_specs=[
            pl.BlockSpec((SUBLANES, tm), lambda i: (0, i)),
            pl.BlockSpec((tm, LANES), lambda i: (i, 0)),
            pl.BlockSpec((1, LANES), lambda i: (0, 0)),
        ],
        out_shape=[
            jax.ShapeDtypeStruct((SUBLANES, t), F32),
            jax.ShapeDtypeStruct((t, LANES), F32),
            jax.ShapeDtypeStruct((1, LANES), F32),
        ],
        scratch_shapes=[pltpu.VMEM((1, LANES), F32)],
        compiler_params=_cparams(1, 32),
        name="router",
    )(x, g.reshape(1, d), wr, tri)


def _dispatch_kernel(zs_ref, pos_ref, x_ref, g_ref, xs_ref, h_ref, zero_ref, sem, zsem, *, n_exp):
    tm = h_ref.shape[0]
    h_ref[...] = _rms(x_ref[...], g_ref[...])

    @pl.when(pl.program_id(0) == 0)
    def _():
        zero_ref[...] = jnp.zeros_like(zero_ref)
        nz = zero_ref.shape[0]
        n_rows = xs_ref.shape[0]
        for e in range(n_exp):
            start = pl.multiple_of(zs_ref[e], SUBLANES)
            cp = pltpu.make_async_copy(zero_ref, xs_ref.at[pl.ds(start, nz), :], zsem)
            cp.start()
            cp.wait()
            cp = pltpu.make_async_copy(zero_ref.at[pl.ds(0, tm), :],
                                       xs_ref.at[pl.ds(n_rows - (e + 1) * tm, tm), :], zsem)
            cp.start()
            cp.wait()

    for r in range(tm):
        for k in range(TOP_K):
            p = pos_ref[0, 0, k * tm + r]
            pltpu.make_async_copy(h_ref.at[pl.ds(r, 1), :], xs_ref.at[pl.ds(p, 1), :],
                                  sem).start(priority=k)
    for k in range(TOP_K):
        pltpu.make_async_copy(h_ref, xs_ref.at[pl.ds(0, tm), :], sem).wait()


def _dispatch(x, g, pos3, zero_start, n_rows, *, tm):
    t, d = x.shape
    n_exp = zero_start.shape[0]
    grid_spec = pltpu.PrefetchScalarGridSpec(
        num_scalar_prefetch=1,
        grid=(t // tm,),
        in_specs=[
            pl.BlockSpec((1, 1, TOP_K * tm), lambda i, zs: (i, 0, 0), memory_space=pltpu.SMEM),
            pl.BlockSpec((tm, d), lambda i, zs: (i, 0)),
            _resident((1, d), lambda i, zs: (0, 0)),
        ],
        out_specs=pl.BlockSpec(memory_space=pl.ANY),
        scratch_shapes=[pltpu.VMEM((tm, d), F32), pltpu.VMEM((tm + SUBLANES, d), F32),
                        pltpu.SemaphoreType.DMA(()), pltpu.SemaphoreType.DMA(())],
    )
    return pl.pallas_call(
        functools.partial(_dispatch_kernel, n_exp=n_exp),
        grid_spec=grid_spec,
        out_shape=jax.ShapeDtypeStruct((n_rows, d), F32),
        compiler_params=_cparams(1, 32),
        name="dispatch",
    )(zero_start, pos3, x, g.reshape(1, d))


def _gather_rows(pos_ref, tile, ys_ref, buf_ref, sem, slot, tm):
    for r in range(tm):
        for k in range(TOP_K):
            p = pos_ref[tile, 0, k * tm + r]
            pltpu.make_async_copy(ys_ref.at[pl.ds(p, 1), :], buf_ref.at[slot, k, pl.ds(r, 1), :],
                                  sem.at[slot]).start(priority=k)


def _combine_kernel(pos_ref, posn_ref, x_ref, gate_ref, gf_ref, ys_ref, *rest, n_prompt_steps):
    if n_prompt_steps is None:
        (o_ref,), (buf_ref, sem) = rest[:1], rest[1:]
    else:
        (op_ref, os_ref), (buf_ref, sem) = rest[:2], rest[2:]
    i = pl.program_id(0)
    n = pl.num_programs(0)
    tm = x_ref.shape[0] // 2

    def combine(slot):
        for k in range(TOP_K):
            pltpu.make_async_copy(ys_ref.at[pl.ds(0, tm), :], buf_ref.at[slot, k], sem.at[slot]).wait()
        rows = slice(slot * tm, (slot + 1) * tm)
        gate = gate_ref[rows, :]
        y = x_ref[rows, :] + gate[:, 0:1] * buf_ref[slot, 0] + gate[:, 1:2] * buf_ref[slot, 1]
        if n_prompt_steps is None:
            o_ref[rows, :] = y
        else:
            y = _rms(y, gf_ref[...])

            @pl.when(i < n_prompt_steps)
            def _():
                op_ref[rows, :] = y

            @pl.when(i >= n_prompt_steps)
            def _():
                os_ref[rows, :] = y

    @pl.when(i == 0)
    def _():
        _gather_rows(pos_ref, 0, ys_ref, buf_ref, sem, 0, tm)

    _gather_rows(pos_ref, 1, ys_ref, buf_ref, sem, 1, tm)
    combine(0)

    @pl.when(i + 1 < n)
    def _():
        _gather_rows(posn_ref, 0, ys_ref, buf_ref, sem, 0, tm)

    combine(1)


def _combine(x, gates, pos3, ys, g_final, *, tm, t_prompt=None):
    t, d = x.shape
    ns = t // (2 * tm)
    if t_prompt is None:
        nps = None
        out_specs = pl.BlockSpec((2 * tm, d), lambda i: (i, 0))
        out_shape = jax.ShapeDtypeStruct((t, d), F32)
    else:
        nps = t_prompt // (2 * tm)
        out_specs = [pl.BlockSpec((2 * tm, d), lambda i: (jnp.minimum(i, nps - 1), 0)),
                     pl.BlockSpec((2 * tm, d), lambda i: (jnp.maximum(i - nps, 0), 0))]
        out_shape = [jax.ShapeDtypeStruct((t_prompt, d), F32),
                     jax.ShapeDtypeStruct((t - t_prompt, d), F32)]
    return pl.pallas_call(
        functools.partial(_combine_kernel, n_prompt_steps=nps),
        grid=(ns,),
        in_specs=[
            pl.BlockSpec((2, 1, TOP_K * tm), lambda i: (i, 0, 0), memory_space=pltpu.SMEM),
            pl.BlockSpec((2, 1, TOP_K * tm), lambda i: (jnp.minimum(i + 1, ns - 1), 0, 0),
                         memory_space=pltpu.SMEM),
            pl.BlockSpec((2 * tm, d), lambda i: (i, 0)),
            pl.BlockSpec((2 * tm, LANES), lambda i: (i, 0)),
            _resident((1, d), lambda i: (0, 0)),
            pl.BlockSpec(memory_space=pl.ANY),
        ],
        out_specs=out_specs,
        out_shape=out_shape,
        scratch_shapes=[pltpu.VMEM((2, TOP_K, tm, d), F32), pltpu.SemaphoreType.DMA((2,))],
        compiler_params=_cparams(1, 48),
        name="combine",
    )(pos3, pos3, x, gates, g_final.reshape(1, d), ys)


def _moe_layer(x, g, w_router, wg, wu, wd, layer, g_final, *, tm, tf, t_prompt=None):
    t, d = x.shape
    n_exp = w_router.shape[1]
    meta, gates, cnt = _router(x, g, w_router, tm=tm)

    counts = cnt[0, :n_exp].astype(jnp.int32)
    padded = ((counts + tm - 1) // tm) * tm
    ends = jnp.cumsum(padded)
    base = ends - padded
    meta = meta.astype(jnp.int32)

    def positions(e, r):
        p = r
        for q in range(n_exp):
            p = p + jnp.where(e == q, base[q], 0)
        return p

    pos = jnp.stack([positions(meta[0], meta[2]), positions(meta[1], meta[3])], axis=0)
    pos3 = pos.reshape(TOP_K, t // tm, tm).transpose(1, 0, 2).reshape(t // tm, 1, TOP_K * tm)

    n_rows = TOP_K * t + n_exp * tm
    n_tiles = n_rows // tm
    tile_start = jnp.arange(n_tiles, dtype=jnp.int32) * tm
    tile_expert = jnp.minimum(
        jnp.sum((tile_start[:, None] >= ends[None, :]).astype(jnp.int32), axis=1), n_exp - 1)
    n_used = (ends[-1:] // tm).astype(jnp.int32)

    zero_start = (base + counts) // SUBLANES * SUBLANES
    xs = _dispatch(x, g, pos3, zero_start, n_rows, tm=tm)
    ys = _swiglu(xs, g, wg, wu, wd, layer, tile_expert, n_used, tm=tm, tf=tf, dense=False)
    return _combine(x, gates, pos3, ys, g_final, tm=tm, t_prompt=t_prompt)


def _fnet_fold_kernel(cs_ref, w_ref, o_ref):
    y = jnp.dot(cs_ref[...], w_ref[...], precision=lax.Precision.HIGHEST,
                preferred_element_type=F32)
    gd = w_ref.shape[0]
    o_ref[0, 0] = y[:gd].astype(BF)
    o_ref[1, 0] = y[gd:].astype(BF)


def _fnet_fold(w_out):
    d = w_out.shape[0]
    gd = d // FNET_GROUPS
    kk = (jnp.arange(gd, dtype=jnp.int32)[:, None] * jnp.arange(gd, dtype=jnp.int32)[None, :]) % gd
    ang = kk.astype(F32) * (2.0 * math.pi / gd)
    cs = jnp.concatenate([jnp.cos(ang), -jnp.sin(ang)], axis=0) * (gd ** -0.5)
    out = pl.pallas_call(
        _fnet_fold_kernel,
        grid=(FNET_GROUPS,),
        in_specs=[_resident((2 * gd, gd), lambda i: (0, 0)),
                  pl.BlockSpec((gd, d), lambda i: (i, 0))],
        out_specs=pl.BlockSpec((2, 1, gd, d), lambda i: (0, i, 0, 0)),
        out_shape=jax.ShapeDtypeStruct((2, FNET_GROUPS, gd, d), BF),
        compiler_params=_cparams(1, 32),
        name="fnet_fold",
    )(cs, w_out)
    return out.reshape(2 * d, d)


def _fnet1_kernel(x_ref, g_ref, t_ref, rep_ref, z_ref):
    n1, nsub, d = x_ref.shape
    rows = n1 * nsub
    h = _rms(x_ref[...].reshape(rows, d), g_ref[...]).astype(BF)
    r = lax.broadcasted_iota(jnp.int32, (rows, rows), 0)
    c = lax.broadcasted_iota(jnp.int32, (rows, rows), 1)
    same_n2 = (r % nsub) == (c % nsub)
    for ri in range(2):
        rep = _dot(rep_ref[...], t_ref[0, ri * n1:(ri + 1) * n1, :])
        m = jnp.where(same_n2, rep, 0.0).astype(BF)
        z_ref[ri] = _dot(m, h).reshape(n1, nsub, d)


def _fnet2_kernel(z_ref, x_ref, m3_ref, wf_ref, o_ref, ab_ref):
    n2, kb, d = x_ref.shape
    m3 = m3_ref[...]
    for j in range(kb):
        zz = jnp.concatenate([z_ref[0, j], z_ref[1, j]], axis=0).astype(BF)
        ab = _dot(m3, zz)
        ab_ref[j * n2:(j + 1) * n2, :d] = ab[:n2].astype(BF)
        ab_ref[j * n2:(j + 1) * n2, d:] = ab[n2:].astype(BF)
    y = _dot(ab_ref[...], wf_ref[...])
    for j in range(kb):
        o_ref[:, j, :] = x_ref[:, j, :] + y[j * n2:(j + 1) * n2]


def _fnet_tables(seq, nsub):
    n2 = FNET_N2
    n1 = seq // n2
    steps = n2 // nsub
    k1 = jnp.arange(n1, dtype=jnp.int32)
    nn = n2 * jnp.arange(n1, dtype=jnp.int32)[:, None] + jnp.arange(n2, dtype=jnp.int32)[None, :]
    ang1 = ((k1[:, None, None] * nn[None]) % seq).astype(F32) * (2.0 * math.pi / seq)
    tab = jnp.stack([jnp.cos(ang1), jnp.sin(ang1)], axis=0) * (n1 ** -0.5)
    m1 = tab.reshape(2, n1, n1, steps, nsub).transpose(3, 0, 1, 2, 4).reshape(
        steps, 2 * n1, n1 * nsub).astype(BF)
    rep = (jnp.arange(n1 * nsub, dtype=jnp.int32)[:, None] // nsub
           == jnp.arange(n1, dtype=jnp.int32)[None, :]).astype(BF)
    q = jnp.arange(n2, dtype=jnp.int32)
    ang3 = ((q[:, None] * q[None, :]) % n2).astype(F32) * (2.0 * math.pi / n2)
    c3, s3 = jnp.cos(ang3), jnp.sin(ang3)
    m3 = (jnp.concatenate([jnp.concatenate([c3, -s3], axis=1),
                           jnp.concatenate([s3, c3], axis=1)], axis=0) * (n2 ** -0.5)).astype(BF)
    return m1, rep, m3


def _fnet_group(x, g, wfold, *, row0, bsz, seq):
    t, d = x.shape
    n2 = FNET_N2
    n1 = seq // n2
    nsub = min(FNET_ROWS // n1, n2)
    kb = min(SUBLANES, n1)
    m1, rep, m3 = _fnet_tables(seq, nsub)
    rb = row0 // seq

    z = pl.pallas_call(
        _fnet1_kernel,
        grid=(bsz, n2 // nsub),
        in_specs=[
            pl.BlockSpec((n1, nsub, d), lambda b, j: (rb + b, j, 0)),
            _resident((1, d), lambda b, j: (0, 0)),
            pl.BlockSpec((1, 2 * n1, n1 * nsub), lambda b, j: (j, 0, 0)),
            _resident((n1 * nsub, n1), lambda b, j: (0, 0)),
        ],
        out_specs=pl.BlockSpec((None, 2, n1, nsub, d), lambda b, j: (b, 0, 0, j, 0)),
        out_shape=jax.ShapeDtypeStruct((bsz, 2, n1, n2, d), F32),
        compiler_params=_cparams(2, 56),
        name="fnet_stage1",
    )(x.reshape(t // n2, n2, d), g.reshape(1, d), m1, rep)

    out = pl.pallas_call(
        _fnet2_kernel,
        grid=(bsz, n1 // kb),
        in_specs=[
            pl.BlockSpec((None, 2, kb, n2, d), lambda b, j: (b, 0, j, 0, 0)),
            pl.BlockSpec((n2, kb, d), lambda b, j: (rb + b, j, 0)),
            _resident((2 * n2, 2 * n2), lambda b, j: (0, 0)),
            _resident((2 * d, d), lambda b, j: (0, 0)),
        ],
        out_specs=pl.BlockSpec((n2, kb, d), lambda b, j: (rb + b, j, 0)),
        out_shape=jax.ShapeDtypeStruct((t // n1, n1, d), F32),
        scratch_shapes=[pltpu.VMEM((kb * n2, 2 * d), BF)],
        input_output_aliases={1: 0},
        compiler_params=_cparams(2, 56),
        name="fnet_stage2",
    )(z, x.reshape(t // n1, n1, d), m3, wfold)
    return out.reshape(t, d)


def _sgu_kernel(x_ref, g_ref, win_ref, vg_ref, ws_ref, bs_ref, wout_ref, o_ref, v_ref, sv_ref, *, tf):
    tm = x_ref.shape[0]
    half = vg_ref.shape[1]
    hd = half // GMLP_HEADS
    x = x_ref[...]
    h = _rms(x, g_ref[...]).astype(BF)
    sqrt_half = math.sqrt(0.5)

    def gelu(a):
        return 0.5 * a * (1.0 + lax.erf(a * sqrt_half))

    ssq = jnp.zeros((tm, 1), F32)
    for c in range(half // tf):
        vc = gelu(_dot(h, win_ref[:, half + c * tf:half + (c + 1) * tf]))
        sv_ref[:, c * tf:(c + 1) * tf] = vc
        ssq = ssq + jnp.sum(vc * vc, axis=-1, keepdims=True)
    inv = lax.rsqrt(ssq * (1.0 / half) + EPS)
    v_ref[...] = (sv_ref[...] * inv * vg_ref[...]).astype(BF)
    nr = tm // CHUNK
    for hh in range(GMLP_HEADS):
        vv = jnp.concatenate([v_ref[r * CHUNK:(r + 1) * CHUNK, hh * hd:(hh + 1) * hd]
                              for r in range(nr)], axis=1)
        blk = _dot(ws_ref[hh], vv) + bs_ref[hh][:, 0:1]
        for r in range(nr):
            sv_ref[r * CHUNK:(r + 1) * CHUNK, hh * hd:(hh + 1) * hd] = blk[:, r * hd:(r + 1) * hd]
    acc = x
    for c in range(half // tf):
        uc = gelu(_dot(h, win_ref[:, c * tf:(c + 1) * tf]))
        acc = acc + _dot((uc * sv_ref[:, c * tf:(c + 1) * tf]).astype(BF),
                         wout_ref[c * tf:(c + 1) * tf, :])
    o_ref[...] = acc


def _sgu_mixer(x, g, w_in, v_gain, w_s, b_s, w_out, *, tm, tf):
    t, d = x.shape
    half = v_gain.shape[0]
    bs = jnp.broadcast_to(b_s[:, :, None], (GMLP_HEADS, CHUNK, LANES)).astype(F32)
    return pl.pallas_call(
        functools.partial(_sgu_kernel, tf=tf),
        grid=(t // tm,),
        in_specs=[
            pl.BlockSpec((tm, d), lambda i: (i, 0)),
            _resident((1, d), lambda i: (0, 0)),
            _resident((d, 2 * half), lambda i: (0, 0)),
            _resident((1, half), lambda i: (0, 0)),
            _resident((GMLP_HEADS, CHUNK, CHUNK), lambda i: (0, 0, 0)),
            _resident((GMLP_HEADS, CHUNK, LANES), lambda i: (0, 0, 0)),
            _resident((half, d), lambda i: (0, 0)),
        ],
        out_specs=pl.BlockSpec((tm, d), lambda i: (i, 0)),
        out_shape=jax.ShapeDtypeStruct((t, d), F32),
        scratch_shapes=[pltpu.VMEM((tm, half), BF), pltpu.VMEM((tm, half), F32)],
        compiler_params=_cparams(1, 56),
        name="sgu_mixer",
    )(x, g.reshape(1, d), w_in.astype(BF), v_gain.reshape(1, half), w_s.astype(BF), bs,
      w_out.astype(BF))


def _final_norm_kernel(x_ref, g_ref, o_ref):
    o_ref[...] = _rms(x_ref[...], g_ref[...])


def _final_norm(x, g, *, tm):
    t, d = x.shape
    return pl.pallas_call(
        _final_norm_kernel,
        grid=(t // tm,),
        in_specs=[pl.BlockSpec((tm, d), lambda i: (i, 0)), _resident((1, d), lambda i: (0, 0))],
        out_specs=pl.BlockSpec((tm, d), lambda i: (i, 0)),
        out_shape=jax.ShapeDtypeStruct((t, d), F32),
        compiler_params=_cparams(1, 32),
        name="final_norm",
    )(x, g.reshape(1, d))


def _tile(n, want):
    while n % want:
        want //= 2
    return want


def kernel(x_prompt, x_sample, norm_mix, norm_ffn, norm_final, conv_w_in, conv_w, conv_w_out, fnet_w_out, sgu_w_in, sgu_v_gain, sgu_w_s, sgu_b_s, sgu_w_out, ffn_w_gate, ffn_w_up, ffn_w_down, moe_w_router, moe_w_gate, moe_w_up, moe_w_down):
    bp, sp, d = x_prompt.shape
    bs_, ss, _ = x_sample.shape
    t_prompt = bp * sp
    t = t_prompt + bs_ * ss
    depth = norm_mix.shape[0]
    f = ffn_w_gate.shape[-1]
    tf = f // 7 if f % 7 == 0 else _tile(f, 512)
    tm = _tile(math.gcd(sp, ss), 512)
    half = sgu_v_gain.shape[-1]

    zero1 = jnp.zeros((t // tm,), jnp.int32)
    all_tiles = jnp.full((1,), t // tm, jnp.int32)
    ffn_w = [w.astype(BF)[:, None] for w in (ffn_w_gate, ffn_w_up, ffn_w_down)]
    moe_w = [w.astype(BF) for w in (moe_w_gate, moe_w_up, moe_w_down)]

    x = None
    for i in range(depth):
        m, j = i % N_MIXERS, i // N_MIXERS
        if m == 0:
            src = (x, None) if i else (x_prompt.reshape(t_prompt, d), x_sample.reshape(t - t_prompt, d))
            x = _conv_mixer(*src, norm_mix[i], conv_w_in[j], conv_w[j], conv_w_out[j], tm=tm,
                            t_prompt=t_prompt, s_prompt=sp, s_sample=ss)
        elif m == 1:
            wfold = _fnet_fold(fnet_w_out[j])
            x = _fnet_group(x, norm_mix[i], wfold, row0=0, bsz=bp, seq=sp)
            x = _fnet_group(x, norm_mix[i], wfold, row0=t_prompt, bsz=bs_, seq=ss)
        else:
            x = _sgu_mixer(x, norm_mix[i], sgu_w_in[j], sgu_v_gain[j], sgu_w_s[j], sgu_b_s[j],
                           sgu_w_out[j], tm=tm, tf=_tile(half, 512))
        k = i // 2
        last = i == depth - 1
        if i % 2 == 0:
            x = _swiglu(x, norm_ffn[i], *ffn_w, k, zero1, all_tiles, tm=tm, tf=tf, dense=True)
            if last:
                x = _final_norm(x, norm_final, tm=tm)
                x = (x[:t_prompt], x[t_prompt:])
        else:
            x = _moe_layer(x, norm_ffn[i], moe_w_router[k], *moe_w, k, norm_final, tm=tm, tf=tf,
                           t_prompt=t_prompt if last else None)
    return (x[0].reshape(bp, sp, d), x[1].reshape(bs_, ss, d))
```

```python
import functools
import math

import jax
import jax.numpy as jnp
from jax import lax
from jax.experimental import pallas as pl
from jax.experimental.pallas import tpu as pltpu

EPS = 1e-6
BF = jnp.bfloat16
F32 = jnp.float32

N_MIXERS = 3
FNET_GROUPS = 8
GMLP_HEADS = 8
CHUNK = 128
TOP_K = 2
LANES = 128
SUBLANES = 8
HALO = 16
FNET_N2 = 128
FNET_ROWS = 1024


def _cparams(n_axes, vmem_mb):
    return pltpu.CompilerParams(
        dimension_semantics=("arbitrary",) * n_axes,
        vmem_limit_bytes=vmem_mb << 20,
    )


def _rms(x, g):
    ms = jnp.mean(x * x, axis=-1, keepdims=True)
    return x * lax.rsqrt(ms + EPS) * g


def _dot(a, b):
    return jnp.dot(a, b, preferred_element_type=F32)


def _resident(shape, index_map):
    return pl.BlockSpec(shape, index_map, pipeline_mode=pl.Buffered(1))


def _cast_specs(w, steps):
    rows, cols = w.shape
    nblk = 1
    while nblk * 2 <= steps and rows % (nblk * 2 * 16) == 0:
        nblk *= 2
    spec = pl.BlockSpec((rows // nblk, cols), lambda i, *_: (jnp.minimum(i, nblk - 1), 0))
    return spec, spec, jax.ShapeDtypeStruct((rows, cols), BF)


def _cast_blocks(cast_in, cast_out):
    for src, dst in zip(cast_in, cast_out):
        dst[...] = src[...].astype(BF)


def _conv_kernel(*refs, tm, t_prompt, s_prompt, s_sample, two_sources, n_cast):
    n_src = 6 if two_sources else 3
    g_ref, win_ref, cw_ref, wout_ref = refs[n_src:n_src + 4]
    cast_in = refs[n_src + 4:n_src + 4 + n_cast]
    o_ref = refs[n_src + 4 + n_cast]
    cast_out = refs[n_src + 5 + n_cast:n_src + 5 + 2 * n_cast]
    h_ref, cz_ref = refs[-2:]
    _cast_blocks(cast_in, cast_out)
    i = pl.program_id(0)
    d = o_ref.shape[1]
    g = g_ref[...]
    if two_sources:
        in_prompt = i * tm < t_prompt
        xp, xc, xn = (jnp.where(in_prompt, a[...], b[...]) for a, b in zip(refs[0:3], refs[3:6]))
    else:
        xp, xc, xn = (a[...] for a in refs[0:3])

    def seq_boundary(r):
        return jnp.where(r <= t_prompt, r % s_prompt == 0, (r - t_prompt) % s_sample == 0)

    first = seq_boundary(i * tm)
    last = seq_boundary(i * tm + tm)
    h_ref[0:HALO, :] = jnp.where(first, 0.0, _rms(xp, g)).astype(BF)
    h_ref[HALO:HALO + tm, :] = _rms(xc, g).astype(BF)
    h_ref[HALO + tm:, :] = jnp.where(last, 0.0, _rms(xn, g)).astype(BF)

    hext = h_ref[...]
    c = _dot(hext, win_ref[:, d:2 * d])
    z = _dot(hext, win_ref[:, 2 * d:])
    cz_ref[...] = c * z
    b = _dot(h_ref[HALO:HALO + tm, :], win_ref[:, :d])
    cw = cw_ref[...]
    conv = (cz_ref[HALO - 1:HALO - 1 + tm, :] * cw[0:1, :]
            + cz_ref[HALO:HALO + tm, :] * cw[1:2, :]
            + cz_ref[HALO + 1:HALO + 1 + tm, :] * cw[2:3, :])
    y = _dot((b * conv).astype(BF), wout_ref[...])
    o_ref[...] = xc + y


def _conv_mixer(x, x_sample, g, w_in, conv_w, w_out, *, tm, t_prompt, s_prompt, s_sample, cast=()):
    d = x.shape[1]
    xs = [x] if x_sample is None else [x, x_sample]
    t = sum(a.shape[0] for a in xs)
    nh = tm // HALO
    kern = functools.partial(_conv_kernel, tm=tm, t_prompt=t_prompt, s_prompt=s_prompt,
                             s_sample=s_sample, two_sources=len(xs) == 2, n_cast=len(cast))
    cast_specs = [_cast_specs(w, t // tm) for w in cast]

    def source_specs(rows, tile0):
        nt, nhb = rows // tm, rows // HALO
        cur = lambda i: jnp.clip(i - tile0, 0, nt - 1)
        return [
            pl.BlockSpec((HALO, d), lambda i: (jnp.maximum(cur(i) * nh - 1, 0), 0)),
            pl.BlockSpec((tm, d), lambda i: (cur(i), 0)),
            pl.BlockSpec((HALO, d), lambda i: (jnp.minimum((cur(i) + 1) * nh, nhb - 1), 0)),
        ]

    if len(xs) == 2:
        in_specs = source_specs(t_prompt, 0) + source_specs(t - t_prompt, t_prompt // tm)
        args = [xs[0]] * 3 + [xs[1]] * 3
    else:
        in_specs = source_specs(t, 0)
        args = [xs[0]] * 3
    in_specs += [
        _resident((1, d), lambda i: (0, 0)),
        _resident((d, 3 * d), lambda i: (0, 0)),
        _resident((3, d), lambda i: (0, 0)),
        _resident((d, d), lambda i: (0, 0)),
    ]
    args += [g.reshape(1, d), w_in.astype(BF), conv_w, w_out.astype(BF)]
    out = pl.pallas_call(
        kern,
        grid=(t // tm,),
        in_specs=in_specs + [s[0] for s in cast_specs],
        out_specs=[pl.BlockSpec((tm, d), lambda i: (i, 0))] + [s[1] for s in cast_specs],
        out_shape=[jax.ShapeDtypeStruct((t, d), F32)] + [s[2] for s in cast_specs],
        scratch_shapes=[pltpu.VMEM((tm + 2 * HALO, d), BF),
                        pltpu.VMEM((tm + 2 * HALO, d), F32)],
        compiler_params=_cparams(1, 56),
        name="conv_mixer",
    )(*args, *cast)
    return out[0], out[1:]


def _swiglu_kernel(te_ref, nu_ref, x_ref, g_ref, wg_ref, wu_ref, wd_ref, *rest, tf, dense):
    n_cast = len(rest) // 2
    o_ref = rest[n_cast]
    _cast_blocks(rest[:n_cast], rest[n_cast + 1:])
    i = pl.program_id(0)

    @pl.when(i < nu_ref[0])
    def _():
        x = x_ref[...]
        h = (_rms(x, g_ref[...]) if dense else x).astype(BF)
        f = wg_ref.shape[-1]
        acc = x if dense else jnp.zeros_like(x)
        for c in range(f // tf):
            a = _dot(h, wg_ref[:, c * tf:(c + 1) * tf])
            b = _dot(h, wu_ref[:, c * tf:(c + 1) * tf])
            act = (jax.nn.silu(a) * b).astype(BF)
            acc = acc + _dot(act, wd_ref[c * tf:(c + 1) * tf, :])
        o_ref[...] = acc

    @pl.when(i >= nu_ref[0])
    def _():
        o_ref[...] = jnp.zeros_like(o_ref)


def _swiglu(x, g, wg, wu, wd, layer, tile_expert, n_used, *, tm, tf, dense, cast=()):
    t, d = x.shape
    f = wg.shape[-1]
    kern = functools.partial(_swiglu_kernel, tf=tf, dense=dense)
    cast_specs = [_cast_specs(w, t // tm) for w in cast]
    grid_spec = pltpu.PrefetchScalarGridSpec(
        num_scalar_prefetch=2,
        grid=(t // tm,),
        in_specs=[
            pl.BlockSpec((tm, d), lambda i, te, nu: (i, 0)),
            _resident((1, d), lambda i, te, nu: (0, 0)),
            _resident((None, None, d, f), lambda i, te, nu: (layer, te[i], 0, 0)),
            _resident((None, None, d, f), lambda i, te, nu: (layer, te[i], 0, 0)),
            _resident((None, None, f, d), lambda i, te, nu: (layer, te[i], 0, 0)),
        ] + [s[0] for s in cast_specs],
        out_specs=[pl.BlockSpec((tm, d), lambda i, te, nu: (i, 0))] + [s[1] for s in cast_specs],
    )
    out = pl.pallas_call(
        kern,
        grid_spec=grid_spec,
        out_shape=[jax.ShapeDtypeStruct((t, d), F32)] + [s[2] for s in cast_specs],
        compiler_params=_cparams(1, 56),
        name="swiglu_dense" if dense else "swiglu_routed",
    )(tile_expert, n_used, x, g.reshape(1, d), wg, wu, wd, *cast)
    return out[0], out[1:]


def _router_kernel(x_ref, g_ref, wr_ref, tri_ref, meta_ref, gate_ref, cnt_ref,
                   carry_ref, *, n_exp):
    i = pl.program_id(0)

    @pl.when(i == 0)
    def _():
        carry_ref[...] = jnp.zeros_like(carry_ref)

    h = _rms(x_ref[...], g_ref[...])
    h_hi = h.astype(BF)
    h_lo = (h - h_hi.astype(F32)).astype(BF)
    hw = _dot(h_hi, wr_ref[...])
    logits = hw[:, :LANES] + hw[:, LANES:] + _dot(h_lo, wr_ref[:, :LANES])
    lane = lax.broadcasted_iota(jnp.int32, logits.shape, 1).astype(F32)
    neg = jnp.float32(-jnp.inf)
    lg = jnp.where(lane < n_exp, logits, neg)
    v1 = jnp.max(lg, axis=1, keepdims=True)
    i1 = jnp.min(jnp.where(lg == v1, lane, float(LANES)), axis=1, keepdims=True)
    lg2 = jnp.where(lane == i1, neg, lg)
    v2 = jnp.max(lg2, axis=1, keepdims=True)
    i2 = jnp.min(jnp.where(lg2 == v2, lane, float(LANES)), axis=1, keepdims=True)
    e2 = jnp.exp(v2 - v1)
    g1 = 1.0 / (1.0 + e2)
    g2 = e2 / (1.0 + e2)

    sel = jnp.where((lane == i1) | (lane == i2), 1.0, 0.0)
    cum = _dot(tri_ref[...], sel.astype(BF)) + carry_ref[...]
    r1 = jnp.sum(jnp.where(lane == i1, cum, 0.0), axis=1, keepdims=True)
    r2 = jnp.sum(jnp.where(lane == i2, cum, 0.0), axis=1, keepdims=True)
    carry_ref[...] = carry_ref[...] + jnp.sum(sel, axis=0, keepdims=True)
    cnt_ref[...] = carry_ref[...]

    meta = jnp.where(lane == 0, i1, jnp.where(lane == 1, i2, jnp.where(
        lane == 2, r1, jnp.where(lane == 3, r2, 0.0))))
    meta_ref[...] = meta.T[0:SUBLANES, :]
    gate_ref[...] = jnp.where(lane == 0, g1, jnp.where(lane == 1, g2, 0.0))


def _router(x, g, w_router, *, tm):
    t, d = x.shape
    n_exp = w_router.shape[1]
    wr = jnp.zeros((d, LANES), F32).at[:, :n_exp].set(w_router.astype(F32))
    wr_hi = wr.astype(BF)
    wr = jnp.concatenate([wr_hi, (wr - wr_hi.astype(F32)).astype(BF)], axis=1)
    row = lax.broadcasted_iota(jnp.int32, (tm, tm), 0)
    col = lax.broadcasted_iota(jnp.int32, (tm, tm), 1)
    tri = (col < row).astype(BF)
    kern = functools.partial(_router_kernel, n_exp=n_exp)
    return pl.pallas_call(
        kern,
        grid=(t // tm,),
        in_specs=[
            pl.BlockSpec((tm, d), lambda i: (i, 0)),
            _resident((1, d), lambda i: (0, 0)),
            _resident((d, 2 * LANES), lambda i: (0, 0)),
            _resident((tm, tm), lambda i: (0, 0)),
        ],
        out_specs=[
            pl.BlockSpec((SUBLANES, tm), lambda i: (0, i)),
            pl.BlockSpec((tm, LANES), lambda i: (i, 0)),
            pl.BlockSpec((1, LANES), lambda i: (0, 0)),
        ],
        out_shape=[
            jax.ShapeDtypeStruct((SUBLANES, t), F32),
            jax.ShapeDtypeStruct((t, LANES), F32),
            jax.ShapeDtypeStruct((1, LANES), F32),
        ],
        scratch_shapes=[pltpu.VMEM((1, LANES), F32)],
        compiler_params=_cparams(1, 32),
        name="router",
    )(x, g.reshape(1, d), wr, tri)


def _dispatch_kernel(zs_ref, pos_ref, x_ref, g_ref, xs_ref, h_ref, zero_ref, sem, zsem, *, n_exp):
    tm = h_ref.shape[1]
    i = pl.program_id(0)

    @pl.when(i == 0)
    def _():
        zero_ref[...] = jnp.zeros_like(zero_ref)
        nz = zero_ref.shape[0]
        n_rows = xs_ref.shape[0]
        for e in range(n_exp):
            start = pl.multiple_of(zs_ref[e], SUBLANES)
            cp = pltpu.make_async_copy(zero_ref, xs_ref.at[pl.ds(start, nz), :], zsem)
            cp.start()
            cp.wait()
            cp = pltpu.make_async_copy(zero_ref.at[pl.ds(0, tm), :],
                                       xs_ref.at[pl.ds(n_rows - (e + 1) * tm, tm), :], zsem)
            cp.start()
            cp.wait()

    def scatter(s):
        h_ref[s] = _rms(x_ref[s * tm:(s + 1) * tm, :], g_ref[...])
        for r in range(tm):
            for k in range(TOP_K):
                p = pos_ref[s, 0, k * tm + r]
                pltpu.make_async_copy(h_ref.at[s, pl.ds(r, 1), :], xs_ref.at[pl.ds(p, 1), :],
                                      sem.at[s]).start(priority=k)

    def drain(s):
        for k in range(TOP_K):
            pltpu.make_async_copy(h_ref.at[s], xs_ref.at[pl.ds(0, tm), :], sem.at[s]).wait()

    scatter(0)

    @pl.when(i > 0)
    def _():
        drain(1)

    scatter(1)
    drain(0)

    @pl.when(i == pl.num_programs(0) - 1)
    def _():
        drain(1)


def _dispatch(x, g, pos3, zero_start, n_rows, *, tm):
    t, d = x.shape
    n_exp = zero_start.shape[0]
    grid_spec = pltpu.PrefetchScalarGridSpec(
        num_scalar_prefetch=1,
        grid=(t // (2 * tm),),
        in_specs=[
            pl.BlockSpec((2, 1, TOP_K * tm), lambda i, zs: (i, 0, 0), memory_space=pltpu.SMEM),
            pl.BlockSpec((2 * tm, d), lambda i, zs: (i, 0)),
            _resident((1, d), lambda i, zs: (0, 0)),
        ],
        out_specs=pl.BlockSpec(memory_space=pl.ANY),
        scratch_shapes=[pltpu.VMEM((2, tm, d), F32), pltpu.VMEM((tm + SUBLANES, d), F32),
                        pltpu.SemaphoreType.DMA((2,)), pltpu.SemaphoreType.DMA(())],
    )
    return pl.pallas_call(
        functools.partial(_dispatch_kernel, n_exp=n_exp),
        grid_spec=grid_spec,
        out_shape=jax.ShapeDtypeStruct((n_rows, d), F32),
        compiler_params=_cparams(1, 32),
        name="dispatch",
    )(zero_start, pos3, x, g.reshape(1, d))


def _gather_rows(pos_ref, tile, ys_ref, buf_ref, sem, slot, tm):
    for r in range(tm):
        for k in range(TOP_K):
            p = pos_ref[tile, 0, k * tm + r]
            pltpu.make_async_copy(ys_ref.at[pl.ds(p, 1), :], buf_ref.at[slot, k, pl.ds(r, 1), :],
                                  sem.at[slot]).start(priority=k)


def _combine_kernel(pos_ref, posn_ref, x_ref, gate_ref, gf_ref, ys_ref, *rest, n_prompt_steps):
    if n_prompt_steps is None:
        (o_ref,), (buf_ref, sem) = rest[:1], rest[1:]
    else:
        (op_ref, os_ref), (buf_ref, sem) = rest[:2], rest[2:]
    i = pl.program_id(0)
    n = pl.num_programs(0)
    tm = x_ref.shape[0] // 2

    def combine(slot):
        for k in range(TOP_K):
            pltpu.make_async_copy(ys_ref.at[pl.ds(0, tm), :], buf_ref.at[slot, k], sem.at[slot]).wait()
        rows = slice(slot * tm, (slot + 1) * tm)
        gate = gate_ref[rows, :]
        y = x_ref[rows, :] + gate[:, 0:1] * buf_ref[slot, 0] + gate[:, 1:2] * buf_ref[slot, 1]
        if n_prompt_steps is None:
            o_ref[rows, :] = y
        else:
            y = _rms(y, gf_ref[...])

            @pl.when(i < n_prompt_steps)
            def _():
                op_ref[rows, :] = y

            @pl.when(i >= n_prompt_steps)
            def _():
                os_ref[rows, :] = y

    @pl.when(i == 0)
    def _():
        _gather_rows(pos_ref, 0, ys_ref, buf_ref, sem, 0, tm)

    _gather_rows(pos_ref, 1, ys_ref, buf_ref, sem, 1, tm)
    combine(0)

    @pl.when(i + 1 < n)
    def _():
        _gather_rows(posn_ref, 0, ys_ref, buf_ref, sem, 0, tm)

    combine(1)


def _combine(x, gates, pos3, ys, g_final, *, tm, t_prompt=None):
    t, d = x.shape
    ns = t // (2 * tm)
    if t_prompt is None:
        nps = None
        out_specs = pl.BlockSpec((2 * tm, d), lambda i: (i, 0))
        out_shape = jax.ShapeDtypeStruct((t, d), F32)
    else:
        nps = t_prompt // (2 * tm)
        out_specs = [pl.BlockSpec((2 * tm, d), lambda i: (jnp.minimum(i, nps - 1), 0)),
                     pl.BlockSpec((2 * tm, d), lambda i: (jnp.maximum(i - nps, 0), 0))]
        out_shape = [jax.ShapeDtypeStruct((t_prompt, d), F32),
                     jax.ShapeDtypeStruct((t - t_prompt, d), F32)]
    return pl.pallas_call(
        functools.partial(_combine_kernel, n_prompt_steps=nps),
        grid=(ns,),
        in_specs=[
            pl.BlockSpec((2, 1, TOP_K * tm), lambda i: (i, 0, 0), memory_space=pltpu.SMEM),
            pl.BlockSpec((2, 1, TOP_K * tm), lambda i: (jnp.minimum(i + 1, ns - 1), 0, 0),
                         memory_space=pltpu.SMEM),
            pl.BlockSpec((2 * tm, d), lambda i: (i, 0)),
            pl.BlockSpec((2 * tm, LANES), lambda i: (i, 0)),
            _resident((1, d), lambda i: (0, 0)),
            pl.BlockSpec(memory_space=pl.ANY),
        ],
        out_specs=out_specs,
        out_shape=out_shape,
        scratch_shapes=[pltpu.VMEM((2, TOP_K, tm, d), F32), pltpu.SemaphoreType.DMA((2,))],
        compiler_params=_cparams(1, 48),
        name="combine",
    )(pos3, pos3, x, gates, g_final.reshape(1, d), ys)


def _moe_layer(x, g, w_router, wg, wu, wd, layer, g_final, *, tm, tf, t_prompt=None):
    t, d = x.shape
    n_exp = w_router.shape[1]
    meta, gates, cnt = _router(x, g, w_router, tm=tm)

    counts = cnt[0, :n_exp].astype(jnp.int32)
    padded = ((counts + tm - 1) // tm) * tm
    ends = jnp.cumsum(padded)
    base = ends - padded
    meta = meta.astype(jnp.int32)

    def positions(e, r):
        p = r
        for q in range(n_exp):
            p = p + jnp.where(e == q, base[q], 0)
        return p

    pos = jnp.stack([positions(meta[0], meta[2]), positions(meta[1], meta[3])], axis=0)
    pos3 = pos.reshape(TOP_K, t // tm, tm).transpose(1, 0, 2).reshape(t // tm, 1, TOP_K * tm)

    n_rows = TOP_K * t + n_exp * tm
    n_tiles = n_rows // tm
    tile_start = jnp.arange(n_tiles, dtype=jnp.int32) * tm
    tile_expert = jnp.minimum(
        jnp.sum((tile_start[:, None] >= ends[None, :]).astype(jnp.int32), axis=1), n_exp - 1)
    n_used = (ends[-1:] // tm).astype(jnp.int32)

    zero_start = (base + counts) // SUBLANES * SUBLANES
    xs = _dispatch(x, g, pos3, zero_start, n_rows, tm=tm)
    ys, _ = _swiglu(xs, g, wg, wu, wd, layer, tile_expert, n_used, tm=tm, tf=tf, dense=False)
    return _combine(x, gates, pos3, ys, g_final, tm=tm, t_prompt=t_prompt)


def _fnet_fold_kernel(cs_ref, w_ref, o_ref):
    y = jnp.dot(cs_ref[...], w_ref[...], precision=lax.Precision.HIGHEST,
                preferred_element_type=F32)
    gd = w_ref.shape[0]
    o_ref[0, 0] = y[:gd].astype(BF)
    o_ref[1, 0] = y[gd:].astype(BF)


def _fnet_fold(w_out):
    d = w_out.shape[0]
    gd = d // FNET_GROUPS
    kk = (jnp.arange(gd, dtype=jnp.int32)[:, None] * jnp.arange(gd, dtype=jnp.int32)[None, :]) % gd
    ang = kk.astype(F32) * (2.0 * math.pi / gd)
    cs = jnp.concatenate([jnp.cos(ang), -jnp.sin(ang)], axis=0) * (gd ** -0.5)
    out = pl.pallas_call(
        _fnet_fold_kernel,
        grid=(FNET_GROUPS,),
        in_specs=[_resident((2 * gd, gd), lambda i: (0, 0)),
                  pl.BlockSpec((gd, d), lambda i: (i, 0))],
        out_specs=pl.BlockSpec((2, 1, gd, d), lambda i: (0, i, 0, 0)),
        out_shape=jax.ShapeDtypeStruct((2, FNET_GROUPS, gd, d), BF),
        compiler_params=_cparams(1, 32),
        name="fnet_fold",
    )(cs, w_out)
    return out.reshape(2 * d, d)


def _fnet1_kernel(x_ref, g_ref, t_ref, rep_ref, z_ref):
    n1, nsub, d = x_ref.shape
    rows = n1 * nsub
    h = _rms(x_ref[...].reshape(rows, d), g_ref[...]).astype(BF)
    r = lax.broadcasted_iota(jnp.int32, (rows, rows), 0)
    c = lax.broadcasted_iota(jnp.int32, (rows, rows), 1)
    same_n2 = (r % nsub) == (c % nsub)
    for ri in range(2):
        rep = _dot(rep_ref[...], t_ref[0, ri * n1:(ri + 1) * n1, :])
        m = jnp.where(same_n2, rep, 0.0).astype(BF)
        z_ref[ri] = _dot(m, h).reshape(n1, nsub, d)


def _fnet2_kernel(z_ref, x_ref, m3_ref, wf_ref, o_ref, ab_ref):
    n2, kb, d = x_ref.shape
    m3 = m3_ref[...]
    for j in range(kb):
        zz = jnp.concatenate([z_ref[0, j], z_ref[1, j]], axis=0).astype(BF)
        ab = _dot(m3, zz)
        ab_ref[j * n2:(j + 1) * n2, :d] = ab[:n2].astype(BF)
        ab_ref[j * n2:(j + 1) * n2, d:] = ab[n2:].astype(BF)
    y = _dot(ab_ref[...], wf_ref[...])
    for j in range(kb):
        o_ref[:, j, :] = x_ref[:, j, :] + y[j * n2:(j + 1) * n2]


def _fnet_tables(seq, nsub):
    n2 = FNET_N2
    n1 = seq // n2
    steps = n2 // nsub
    k1 = jnp.arange(n1, dtype=jnp.int32)
    nn = n2 * jnp.arange(n1, dtype=jnp.int32)[:, None] + jnp.arange(n2, dtype=jnp.int32)[None, :]
    ang1 = ((k1[:, None, None] * nn[None]) % seq).astype(F32) * (2.0 * math.pi / seq)
    tab = jnp.stack([jnp.cos(ang1), jnp.sin(ang1)], axis=0) * (n1 ** -0.5)
    m1 = tab.reshape(2, n1, n1, steps, nsub).transpose(3, 0, 1, 2, 4).reshape(
        steps, 2 * n1, n1 * nsub).astype(BF)
    rep = (jnp.arange(n1 * nsub, dtype=jnp.int32)[:, None] // nsub
           == jnp.arange(n1, dtype=jnp.int32)[None, :]).astype(BF)
    q = jnp.arange(n2, dtype=jnp.int32)
    ang3 = ((q[:, None] * q[None, :]) % n2).astype(F32) * (2.0 * math.pi / n2)
    c3, s3 = jnp.cos(ang3), jnp.sin(ang3)
    m3 = (jnp.concatenate([jnp.concatenate([c3, -s3], axis=1),
                           jnp.concatenate([s3, c3], axis=1)], axis=0) * (n2 ** -0.5)).astype(BF)
    return m1, rep, m3


def _fnet_group(x, g, wfold, *, row0, bsz, seq):
    t, d = x.shape
    n2 = FNET_N2
    n1 = seq // n2
    nsub = min(FNET_ROWS // n1, n2)
    kb = min(SUBLANES, n1)
    m1, rep, m3 = _fnet_tables(seq, nsub)
    rb = row0 // seq

    z = pl.pallas_call(
        _fnet1_kernel,
        grid=(bsz, n2 // nsub),
        in_specs=[
            pl.BlockSpec((n1, nsub, d), lambda b, j: (rb + b, j, 0)),
            _resident((1, d), lambda b, j: (0, 0)),
            pl.BlockSpec((1, 2 * n1, n1 * nsub), lambda b, j: (j, 0, 0)),
            _resident((n1 * nsub, n1), lambda b, j: (0, 0)),
        ],
        out_specs=pl.BlockSpec((None, 2, n1, nsub, d), lambda b, j: (b, 0, 0, j, 0)),
        out_shape=jax.ShapeDtypeStruct((bsz, 2, n1, n2, d), F32),
        compiler_params=_cparams(2, 56),
        name="fnet_stage1",
    )(x.reshape(t // n2, n2, d), g.reshape(1, d), m1, rep)

    out = pl.pallas_call(
        _fnet2_kernel,
        grid=(bsz, n1 // kb),
        in_specs=[
            pl.BlockSpec((None, 2, kb, n2, d), lambda b, j: (b, 0, j, 0, 0)),
            pl.BlockSpec((n2, kb, d), lambda b, j: (rb + b, j, 0)),
            _resident((2 * n2, 2 * n2), lambda b, j: (0, 0)),
            _resident((2 * d, d), lambda b, j: (0, 0)),
        ],
        out_specs=pl.BlockSpec((n2, kb, d), lambda b, j: (rb + b, j, 0)),
        out_shape=jax.ShapeDtypeStruct((t // n1, n1, d), F32),
        scratch_shapes=[pltpu.VMEM((kb * n2, 2 * d), BF)],
        input_output_aliases={1: 0},
        compiler_params=_cparams(2, 56),
        name="fnet_stage2",
    )(z, x.reshape(t // n1, n1, d), m3, wfold)
    return out.reshape(t, d)


def _sgu_kernel(x_ref, g_ref, win_ref, vg_ref, ws_ref, bs_ref, wout_ref, o_ref, v_ref, sv_ref, *, tf):
    tm = x_ref.shape[0]
    half = vg_ref.shape[1]
    hd = half // GMLP_HEADS
    x = x_ref[...]
    h = _rms(x, g_ref[...]).astype(BF)
    sqrt_half = math.sqrt(0.5)

    def gelu(a):
        return 0.5 * a * (1.0 + lax.erf(a * sqrt_half))

    ssq = jnp.zeros((tm, 1), F32)
    for c in range(half // tf):
        vc = gelu(_dot(h, win_ref[:, half + c * tf:half + (c + 1) * tf]))
        sv_ref[:, c * tf:(c + 1) * tf] = vc
        ssq = ssq + jnp.sum(vc * vc, axis=-1, keepdims=True)
    inv = lax.rsqrt(ssq * (1.0 / half) + EPS)
    v_ref[...] = (sv_ref[...] * inv * vg_ref[...]).astype(BF)
    nr = tm // CHUNK
    for hh in range(GMLP_HEADS):
        vv = jnp.concatenate([v_ref[r * CHUNK:(r + 1) * CHUNK, hh * hd:(hh + 1) * hd]
                              for r in range(nr)], axis=1)
        blk = _dot(ws_ref[hh], vv) + bs_ref[hh][:, 0:1]
        for r in range(nr):
            sv_ref[r * CHUNK:(r + 1) * CHUNK, hh * hd:(hh + 1) * hd] = blk[:, r * hd:(r + 1) * hd]
    acc = x
    for c in range(half // tf):
        uc = gelu(_dot(h, win_ref[:, c * tf:(c + 1) * tf]))
        acc = acc + _dot((uc * sv_ref[:, c * tf:(c + 1) * tf]).astype(BF),
                         wout_ref[c * tf:(c + 1) * tf, :])
    o_ref[...] = acc


def _sgu_mixer(x, g, w_in, v_gain, w_s, b_s, w_out, *, tm, tf):
    t, d = x.shape
    half = v_gain.shape[0]
    bs = jnp.broadcast_to(b_s[:, :, None], (GMLP_HEADS, CHUNK, LANES)).astype(F32)
    return pl.pallas_call(
        functools.partial(_sgu_kernel, tf=tf),
        grid=(t // tm,),
        in_specs=[
            pl.BlockSpec((tm, d), lambda i: (i, 0)),
            _resident((1, d), lambda i: (0, 0)),
            _resident((d, 2 * half), lambda i: (0, 0)),
            _resident((1, half), lambda i: (0, 0)),
            _resident((GMLP_HEADS, CHUNK, CHUNK), lambda i: (0, 0, 0)),
            _resident((GMLP_HEADS, CHUNK, LANES), lambda i: (0, 0, 0)),
            _resident((half, d), lambda i: (0, 0)),
        ],
        out_specs=pl.BlockSpec((tm, d), lambda i: (i, 0)),
        out_shape=jax.ShapeDtypeStruct((t, d), F32),
        scratch_shapes=[pltpu.VMEM((tm, half), BF), pltpu.VMEM((tm, half), F32)],
        compiler_params=_cparams(1, 56),
        name="sgu_mixer",
    )(x, g.reshape(1, d), w_in.astype(BF), v_gain.reshape(1, half), w_s.astype(BF), bs,
      w_out.astype(BF))


def _final_norm_kernel(x_ref, g_ref, o_ref):
    o_ref[...] = _rms(x_ref[...], g_ref[...])


def _final_norm(x, g, *, tm):
    t, d = x.shape
    return pl.pallas_call(
        _final_norm_kernel,
        grid=(t // tm,),
        in_specs=[pl.BlockSpec((tm, d), lambda i: (i, 0)), _resident((1, d), lambda i: (0, 0))],
        out_specs=pl.BlockSpec((tm, d), lambda i: (i, 0)),
        out_shape=jax.ShapeDtypeStruct((t, d), F32),
        compiler_params=_cparams(1, 32),
        name="final_norm",
    )(x, g.reshape(1, d))


def _tile(n, want):
    while n % want:
        want //= 2
    return want


def kernel(x_prompt, x_sample, norm_mix, norm_ffn, norm_final, conv_w_in, conv_w, conv_w_out, fnet_w_out, sgu_w_in, sgu_v_gain, sgu_w_s, sgu_b_s, sgu_w_out, ffn_w_gate, ffn_w_up, ffn_w_down, moe_w_router, moe_w_gate, moe_w_up, moe_w_down):
    bp, sp, d = x_prompt.shape
    bs_, ss, _ = x_sample.shape
    t_prompt = bp * sp
    t = t_prompt + bs_ * ss
    depth = norm_mix.shape[0]
    f = ffn_w_gate.shape[-1]
    tf = f // 7 if f % 7 == 0 else _tile(f, 512)
    tm = _tile(math.gcd(sp, ss), 512)
    half = sgu_v_gain.shape[-1]

    zero1 = jnp.zeros((t // tm,), jnp.int32)
    all_tiles = jnp.full((1,), t // tm, jnp.int32)
    ffn_w = [w.astype(BF)[:, None] for w in (ffn_w_gate, ffn_w_up, ffn_w_down)]
    moe_f32 = (moe_w_gate, moe_w_up, moe_w_down)
    moe_2d = [w.reshape(-1, w.shape[-1]) for w in moe_f32]

    x = None
    for i in range(depth):
        m, j = i % N_MIXERS, i // N_MIXERS
        if m == 0:
            src = (x, None) if i else (x_prompt.reshape(t_prompt, d), x_sample.reshape(t - t_prompt, d))
            x, cast_a = _conv_mixer(*src, norm_mix[i], conv_w_in[j], conv_w[j], conv_w_out[j], tm=tm,
                                    t_prompt=t_prompt, s_prompt=sp, s_sample=ss,
                                    cast=() if i else (moe_2d[0], moe_2d[2]))
        elif m == 1:
            wfold = _fnet_fold(fnet_w_out[j])
            x = _fnet_group(x, norm_mix[i], wfold, row0=0, bsz=bp, seq=sp)
            x = _fnet_group(x, norm_mix[i], wfold, row0=t_prompt, bsz=bs_, seq=ss)
        else:
            x = _sgu_mixer(x, norm_mix[i], sgu_w_in[j], sgu_v_gain[j], sgu_w_s[j], sgu_b_s[j],
                           sgu_w_out[j], tm=tm, tf=_tile(half, 512))
        k = i // 2
        last = i == depth - 1
        if i % 2 == 0:
            x, cast_b = _swiglu(x, norm_ffn[i], *ffn_w, k, zero1, all_tiles, tm=tm, tf=tf, dense=True,
                                cast=() if i else (moe_2d[1],))
            if i == 0:
                moe_w = [w2.reshape(w.shape) for w2, w in zip((cast_a[0], cast_b[0], cast_a[1]), moe_f32)]
            if last:
                x = _final_norm(x, norm_final, tm=tm)
                x = (x[:t_prompt], x[t_prompt:])
        else:
            x = _moe_layer(x, norm_ffn[i], moe_w_router[k], *moe_w, k, norm_final, tm=tm, tf=tf,
                           t_prompt=t_prompt if last else None)
    return (x[0].reshape(bp, sp, d), x[1].reshape(bs_, ss, d))
```

```python
import functools
import math

import jax
import jax.numpy as jnp
from jax import lax
from jax.experimental import pallas as pl
from jax.experimental.pallas import tpu as pltpu

EPS = 1e-6
BF = jnp.bfloat16
F32 = jnp.float32

N_MIXERS = 3
FNET_GROUPS = 8
GMLP_HEADS = 8
CHUNK = 128
TOP_K = 2
LANES = 128
SUBLANES = 8
HALO = 16
FNET_N2 = 128
FNET_ROWS = 1024


def _cparams(n_axes, vmem_mb):
    return pltpu.CompilerParams(
        dimension_semantics=("arbitrary",) * n_axes,
        vmem_limit_bytes=vmem_mb << 20,
    )


def _rms(x, g):
    ms = jnp.mean(x * x, axis=-1, keepdims=True)
    return x * lax.rsqrt(ms + EPS) * g


def _dot(a, b):
    return jnp.dot(a, b, preferred_element_type=F32)


def _resident(shape, index_map):
    return pl.BlockSpec(shape, index_map, pipeline_mode=pl.Buffered(1))


def _cast_specs(w, steps):
    rows, cols = w.shape
    nblk = 1
    while nblk * 2 <= steps and rows % (nblk * 2 * 16) == 0:
        nblk *= 2
    spec = pl.BlockSpec((rows // nblk, cols), lambda i, *_: (jnp.minimum(i, nblk - 1), 0))
    return spec, spec, jax.ShapeDtypeStruct((rows, cols), BF)


def _cast_blocks(cast_in, cast_out):
    for src, dst in zip(cast_in, cast_out):
        dst[...] = src[...].astype(BF)


def _conv_kernel(*refs, tm, t_prompt, s_prompt, s_sample, two_sources, n_cast):
    n_src = 6 if two_sources else 3
    g_ref, win_ref, cw_ref, wout_ref = refs[n_src:n_src + 4]
    cast_in = refs[n_src + 4:n_src + 4 + n_cast]
    o_ref = refs[n_src + 4 + n_cast]
    cast_out = refs[n_src + 5 + n_cast:n_src + 5 + 2 * n_cast]
    h_ref, cz_ref = refs[-2:]
    _cast_blocks(cast_in, cast_out)
    i = pl.program_id(0)
    d = o_ref.shape[1]
    g = g_ref[...]
    if two_sources:
        in_prompt = i * tm < t_prompt
        xp, xc, xn = (jnp.where(in_prompt, a[...], b[...]) for a, b in zip(refs[0:3], refs[3:6]))
    else:
        xp, xc, xn = (a[...] for a in refs[0:3])

    def seq_boundary(r):
        return jnp.where(r <= t_prompt, r % s_prompt == 0, (r - t_prompt) % s_sample == 0)

    first = seq_boundary(i * tm)
    last = seq_boundary(i * tm + tm)
    h_ref[0:HALO, :] = jnp.where(first, 0.0, _rms(xp, g)).astype(BF)
    h_ref[HALO:HALO + tm, :] = _rms(xc, g).astype(BF)
    h_ref[HALO + tm:, :] = jnp.where(last, 0.0, _rms(xn, g)).astype(BF)

    hext = h_ref[...]
    c = _dot(hext, win_ref[:, d:2 * d])
    z = _dot(hext, win_ref[:, 2 * d:])
    cz_ref[...] = c * z
    b = _dot(h_ref[HALO:HALO + tm, :], win_ref[:, :d])
    cw = cw_ref[...]
    conv = (cz_ref[HALO - 1:HALO - 1 + tm, :] * cw[0:1, :]
            + cz_ref[HALO:HALO + tm, :] * cw[1:2, :]
            + cz_ref[HALO + 1:HALO + 1 + tm, :] * cw[2:3, :])
    y = _dot((b * conv).astype(BF), wout_ref[...])
    o_ref[...] = xc + y


def _conv_mixer(x, x_sample, g, w_in, conv_w, w_out, *, tm, t_prompt, s_prompt, s_sample, cast=()):
    d = x.shape[1]
    xs = [x] if x_sample is None else [x, x_sample]
    t = sum(a.shape[0] for a in xs)
    nh = tm // HALO
    kern = functools.partial(_conv_kernel, tm=tm, t_prompt=t_prompt, s_prompt=s_prompt,
                             s_sample=s_sample, two_sources=len(xs) == 2, n_cast=len(cast))
    cast_specs = [_cast_specs(w, t // tm) for w in cast]

    def source_specs(rows, tile0):
        nt, nhb = rows // tm, rows // HALO
        cur = lambda i: jnp.clip(i - tile0, 0, nt - 1)
        return [
            pl.BlockSpec((HALO, d), lambda i: (jnp.maximum(cur(i) * nh - 1, 0), 0)),
            pl.BlockSpec((tm, d), lambda i: (cur(i), 0)),
            pl.BlockSpec((HALO, d), lambda i: (jnp.minimum((cur(i) + 1) * nh, nhb - 1), 0)),
        ]

    if len(xs) == 2:
        in_specs = source_specs(t_prompt, 0) + source_specs(t - t_prompt, t_prompt // tm)
        args = [xs[0]] * 3 + [xs[1]] * 3
    else:
        in_specs = source_specs(t, 0)
        args = [xs[0]] * 3
    in_specs += [
        _resident((1, d), lambda i: (0, 0)),
        _resident((d, 3 * d), lambda i: (0, 0)),
        _resident((3, d), lambda i: (0, 0)),
        _resident((d, d), lambda i: (0, 0)),
    ]
    args += [g.reshape(1, d), w_in.astype(BF), conv_w, w_out.astype(BF)]
    out = pl.pallas_call(
        kern,
        grid=(t // tm,),
        in_specs=in_specs + [s[0] for s in cast_specs],
        out_specs=[pl.BlockSpec((tm, d), lambda i: (i, 0))] + [s[1] for s in cast_specs],
        out_shape=[jax.ShapeDtypeStruct((t, d), F32)] + [s[2] for s in cast_specs],
        scratch_shapes=[pltpu.VMEM((tm + 2 * HALO, d), BF),
                        pltpu.VMEM((tm + 2 * HALO, d), F32)],
        compiler_params=_cparams(1, 56),
        name="conv_mixer",
    )(*args, *cast)
    return out[0], out[1:]


def _swiglu_kernel(te_ref, nu_ref, x_ref, g_ref, wg_ref, wu_ref, wd_ref, *rest, tf, dense):
    n_cast = len(rest) // 2
    o_ref = rest[n_cast]
    _cast_blocks(rest[:n_cast], rest[n_cast + 1:])
    i = pl.program_id(0)

    @pl.when(i < nu_ref[0])
    def _():
        x = x_ref[...]
        h = (_rms(x, g_ref[...]) if dense else x).astype(BF)
        f = wg_ref.shape[-1]
        acc = x if dense else jnp.zeros_like(x)
        for c in range(f // tf):
            a = _dot(h, wg_ref[:, c * tf:(c + 1) * tf])
            b = _dot(h, wu_ref[:, c * tf:(c + 1) * tf])
            act = (jax.nn.silu(a) * b).astype(BF)
            acc = acc + _dot(act, wd_ref[c * tf:(c + 1) * tf, :])
        o_ref[...] = acc

    @pl.when(i >= nu_ref[0])
    def _():
        o_ref[...] = jnp.zeros_like(o_ref)


def _swiglu(x, g, wg, wu, wd, layer, tile_expert, n_used, *, tm, tf, dense, cast=()):
    t, d = x.shape
    f = wg.shape[-1]
    kern = functools.partial(_swiglu_kernel, tf=tf, dense=dense)
    cast_specs = [_cast_specs(w, t // tm) for w in cast]
    prefetched = pl.BlockSpec if not dense else _resident
    grid_spec = pltpu.PrefetchScalarGridSpec(
        num_scalar_prefetch=2,
        grid=(t // tm,),
        in_specs=[
            pl.BlockSpec((tm, d), lambda i, te, nu: (i, 0)),
            _resident((1, d), lambda i, te, nu: (0, 0)),
            prefetched((None, None, d, f), lambda i, te, nu: (layer, te[i], 0, 0)),
            prefetched((None, None, d, f), lambda i, te, nu: (layer, te[i], 0, 0)),
            _resident((None, None, f, d), lambda i, te, nu: (layer, te[i], 0, 0)),
        ] + [s[0] for s in cast_specs],
        out_specs=[pl.BlockSpec((tm, d), lambda i, te, nu: (i, 0))] + [s[1] for s in cast_specs],
    )
    out = pl.pallas_call(
        kern,
        grid_spec=grid_spec,
        out_shape=[jax.ShapeDtypeStruct((t, d), F32)] + [s[2] for s in cast_specs],
        compiler_params=_cparams(1, 56),
        name="swiglu_dense" if dense else "swiglu_routed",
    )(tile_expert, n_used, x, g.reshape(1, d), wg, wu, wd, *cast)
    return out[0], out[1:]


def _router_kernel(x_ref, g_ref, wr_ref, tri_ref, meta_ref, gate_ref, cnt_ref,
                   carry_ref, *, n_exp):
    i = pl.program_id(0)

    @pl.when(i == 0)
    def _():
        carry_ref[...] = jnp.zeros_like(carry_ref)

    h = _rms(x_ref[...], g_ref[...])
    h_hi = h.astype(BF)
    h_lo = (h - h_hi.astype(F32)).astype(BF)
    hw = _dot(h_hi, wr_ref[...])
    logits = hw[:, :LANES] + hw[:, LANES:] + _dot(h_lo, wr_ref[:, :LANES])
    lane = lax.broadcasted_iota(jnp.int32, logits.shape, 1).astype(F32)
    neg = jnp.float32(-jnp.inf)
    lg = jnp.where(lane < n_exp, logits, neg)
    v1 = jnp.max(lg, axis=1, keepdims=True)
    i1 = jnp.min(jnp.where(lg == v1, lane, float(LANES)), axis=1, keepdims=True)
    lg2 = jnp.where(lane == i1, neg, lg)
    v2 = jnp.max(lg2, axis=1, keepdims=True)
    i2 = jnp.min(jnp.where(lg2 == v2, lane, float(LANES)), axis=1, keepdims=True)
    e2 = jnp.exp(v2 - v1)
    g1 = 1.0 / (1.0 + e2)
    g2 = e2 / (1.0 + e2)

    sel = jnp.where((lane == i1) | (lane == i2), 1.0, 0.0)
    cum = _dot(tri_ref[...], sel.astype(BF)) + carry_ref[...]
    r1 = jnp.sum(jnp.where(lane == i1, cum, 0.0), axis=1, keepdims=True)
    r2 = jnp.sum(jnp.where(lane == i2, cum, 0.0), axis=1, keepdims=True)
    carry_ref[...] = carry_ref[...] + jnp.sum(sel, axis=0, keepdims=True)
    cnt_ref[...] = carry_ref[...]

    meta = jnp.where(lane == 0, i1, jnp.where(lane == 1, i2, jnp.where(
        lane == 2, r1, jnp.where(lane == 3, r2, 0.0))))
    meta_ref[...] = meta.T[0:SUBLANES, :]
    gate_ref[...] = jnp.where(lane == 0, g1, jnp.where(lane == 1, g2, 0.0))


def _router(x, g, w_router, *, tm):
    t, d = x.shape
    n_exp = w_router.shape[1]
    wr = jnp.zeros((d, LANES), F32).at[:, :n_exp].set(w_router.astype(F32))
    wr_hi = wr.astype(BF)
    wr = jnp.concatenate([wr_hi, (wr - wr_hi.astype(F32)).astype(BF)], axis=1)
    row = lax.broadcasted_iota(jnp.int32, (tm, tm), 0)
    col = lax.broadcasted_iota(jnp.int32, (tm, tm), 1)
    tri = (col < row).astype(BF)
    kern = functools.partial(_router_kernel, n_exp=n_exp)
    return pl.pallas_call(
        kern,
        grid=(t // tm,),
        in_specs=[
            pl.BlockSpec((tm, d), lambda i: (i, 0)),
            _resident((1, d), lambda i: (0, 0)),
            _resident((d, 2 * LANES), lambda i: (0, 0)),
            _resident((tm, tm), lambda i: (0, 0)),
        ],
        out_specs=[
            pl.BlockSpec((SUBLANES, tm), lambda i: (0, i)),
            pl.BlockSpec((tm, LANES), lambda i: (i, 0)),
            pl.BlockSpec((1, LANES), lambda i: (0, 0)),
        ],
        out_shape=[
            jax.ShapeDtypeStruct((SUBLANES, t), F32),
            jax.ShapeDtypeStruct((t, LANES), F32),
            jax.ShapeDtypeStruct((1, LANES), F32),
        ],
        scratch_shapes=[pltpu.VMEM((1, LANES), F32)],
        compiler_params=_cparams(1, 32),
        name="router",
    )(x, g.reshape(1, d), wr, tri)


def _dispatch_kernel(zs_ref, pos_ref, x_ref, g_ref, xs_ref, h_ref, zero_ref, sem, zsem, *, n_exp):
    tm = h_ref.shape[1]
    i = pl.program_id(0)

    @pl.when(i == 0)
    def _():
        zero_ref[...] = jnp.zeros_like(zero_ref)
        nz = zero_ref.shape[0]
        n_rows = xs_ref.shape[0]
        for e in range(n_exp):
            start = pl.multiple_of(zs_ref[e], SUBLANES)
            cp = pltpu.make_async_copy(zero_ref, xs_ref.at[pl.ds(start, nz), :], zsem)
            cp.start()
            cp.wait()
            cp = pltpu.make_async_copy(zero_ref.at[pl.ds(0, tm), :],
                                       xs_ref.at[pl.ds(n_rows - (e + 1) * tm, tm), :], zsem)
            cp.start()
            cp.wait()

    def scatter(s):
        h_ref[s] = _rms(x_ref[s * tm:(s + 1) * tm, :], g_ref[...])
        for r in range(tm):
            for k in range(TOP_K):
                p = pos_ref[s, 0, k * tm + r]
                pltpu.make_async_copy(h_ref.at[s, pl.ds(r, 1), :], xs_ref.at[pl.ds(p, 1), :],
                                      sem.at[s]).start(priority=k)

    def drain(s):
        for k in range(TOP_K):
            pltpu.make_async_copy(h_ref.at[s], xs_ref.at[pl.ds(0, tm), :], sem.at[s]).wait()

    scatter(0)

    @pl.when(i > 0)
    def _():
        drain(1)

    scatter(1)
    drain(0)

    @pl.when(i == pl.num_programs(0) - 1)
    def _():
        drain(1)


def _dispatch(x, g, pos3, zero_start, n_rows, *, tm):
    t, d = x.shape
    n_exp = zero_start.shape[0]
    grid_spec = pltpu.PrefetchScalarGridSpec(
        num_scalar_prefetch=1,
        grid=(t // (2 * tm),),
        in_specs=[
            pl.BlockSpec((2, 1, TOP_K * tm), lambda i, zs: (i, 0, 0), memory_space=pltpu.SMEM),
            pl.BlockSpec((2 * tm, d), lambda i, zs: (i, 0)),
            _resident((1, d), lambda i, zs: (0, 0)),
        ],
        out_specs=pl.BlockSpec(memory_space=pl.ANY),
        scratch_shapes=[pltpu.VMEM((2, tm, d), F32), pltpu.VMEM((tm + SUBLANES, d), F32),
                        pltpu.SemaphoreType.DMA((2,)), pltpu.SemaphoreType.DMA(())],
    )
    return pl.pallas_call(
        functools.partial(_dispatch_kernel, n_exp=n_exp),
        grid_spec=grid_spec,
        out_shape=jax.ShapeDtypeStruct((n_rows, d), F32),
        compiler_params=_cparams(1, 32),
        name="dispatch",
    )(zero_start, pos3, x, g.reshape(1, d))


def _gather_rows(pos_ref, tile, ys_ref, buf_ref, sem, slot, tm):
    for r in range(tm):
        for k in range(TOP_K):
            p = pos_ref[tile, 0, k * tm + r]
            pltpu.make_async_copy(ys_ref.at[pl.ds(p, 1), :], buf_ref.at[slot, k, pl.ds(r, 1), :],
                                  sem.at[slot]).start(priority=k)


def _combine_kernel(pos_ref, posn_ref, x_ref, gate_ref, gf_ref, ys_ref, *rest, n_prompt_steps):
    if n_prompt_steps is None:
        (o_ref,), (buf_ref, sem) = rest[:1], rest[1:]
    else:
        (op_ref, os_ref), (buf_ref, sem) = rest[:2], rest[2:]
    i = pl.program_id(0)
    n = pl.num_programs(0)
    tm = x_ref.shape[0] // 2

    def combine(slot):
        for k in range(TOP_K):
            pltpu.make_async_copy(ys_ref.at[pl.ds(0, tm), :], buf_ref.at[slot, k], sem.at[slot]).wait()
        rows = slice(slot * tm, (slot + 1) * tm)
        gate = gate_ref[rows, :]
        y = x_ref[rows, :] + gate[:, 0:1] * buf_ref[slot, 0] + gate[:, 1:2] * buf_ref[slot, 1]
        if n_prompt_steps is None:
            o_ref[rows, :] = y
        else:
            y = _rms(y, gf_ref[...])

            @pl.when(i < n_prompt_steps)
            def _():
                op_ref[rows, :] = y

            @pl.when(i >= n_prompt_steps)
            def _():
                os_ref[rows, :] = y

    @pl.when(i == 0)
    def _():
        _gather_rows(pos_ref, 0, ys_ref, buf_ref, sem, 0, tm)

    _gather_rows(pos_ref, 1, ys_ref, buf_ref, sem, 1, tm)
    combine(0)

    @pl.when(i + 1 < n)
    def _():
        _gather_rows(posn_ref, 0, ys_ref, buf_ref, sem, 0, tm)

    combine(1)


def _combine(x, gates, pos3, ys, g_final, *, tm, t_prompt=None):
    t, d = x.shape
    ns = t // (2 * tm)
    if t_prompt is None:
        nps = None
        out_specs = pl.BlockSpec((2 * tm, d), lambda i: (i, 0))
        out_shape = jax.ShapeDtypeStruct((t, d), F32)
    else:
        nps = t_prompt // (2 * tm)
        out_specs = [pl.BlockSpec((2 * tm, d), lambda i: (jnp.minimum(i, nps - 1), 0)),
                     pl.BlockSpec((2 * tm, d), lambda i: (jnp.maximum(i - nps, 0), 0))]
        out_shape = [jax.ShapeDtypeStruct((t_prompt, d), F32),
                     jax.ShapeDtypeStruct((t - t_prompt, d), F32)]
    return pl.pallas_call(
        functools.partial(_combine_kernel, n_prompt_steps=nps),
        grid=(ns,),
        in_specs=[
            pl.BlockSpec((2, 1, TOP_K * tm), lambda i: (i, 0, 0), memory_space=pltpu.SMEM),
            pl.BlockSpec((2, 1, TOP_K * tm), lambda i: (jnp.minimum(i + 1, ns - 1), 0, 0),
                         memory_space=pltpu.SMEM),
            pl.BlockSpec((2 * tm, d), lambda i: (i, 0)),
            pl.BlockSpec((2 * tm, LANES), lambda i: (i, 0)),
            _resident((1, d), lambda i: (0, 0)),
            pl.BlockSpec(memory_space=pl.ANY),
        ],
        out_specs=out_specs,
        out_shape=out_shape,
        scratch_shapes=[pltpu.VMEM((2, TOP_K, tm, d), F32), pltpu.SemaphoreType.DMA((2,))],
        compiler_params=_cparams(1, 48),
        name="combine",
    )(pos3, pos3, x, gates, g_final.reshape(1, d), ys)


def _moe_layer(x, g, w_router, wg, wu, wd, layer, g_final, *, tm, tf, t_prompt=None):
    t, d = x.shape
    n_exp = w_router.shape[1]
    meta, gates, cnt = _router(x, g, w_router, tm=tm)

    counts = cnt[0, :n_exp].astype(jnp.int32)
    padded = ((counts + tm - 1) // tm) * tm
    ends = jnp.cumsum(padded)
    base = ends - padded
    meta = meta.astype(jnp.int32)

    def positions(e, r):
        p = r
        for q in range(n_exp):
            p = p + jnp.where(e == q, base[q], 0)
        return p

    pos = jnp.stack([positions(meta[0], meta[2]), positions(meta[1], meta[3])], axis=0)
    pos3 = pos.reshape(TOP_K, t // tm, tm).transpose(1, 0, 2).reshape(t // tm, 1, TOP_K * tm)

    n_rows = TOP_K * t + n_exp * tm
    n_tiles = n_rows // tm
    tile_start = jnp.arange(n_tiles, dtype=jnp.int32) * tm
    tile_expert = jnp.minimum(
        jnp.sum((tile_start[:, None] >= ends[None, :]).astype(jnp.int32), axis=1), n_exp - 1)
    n_used = (ends[-1:] // tm).astype(jnp.int32)

    zero_start = (base + counts) // SUBLANES * SUBLANES
    xs = _dispatch(x, g, pos3, zero_start, n_rows, tm=tm)
    ys, _ = _swiglu(xs, g, wg, wu, wd, layer, tile_expert, n_used, tm=tm, tf=tf, dense=False)
    return _combine(x, gates, pos3, ys, g_final, tm=tm, t_prompt=t_prompt)


def _fnet_fold_kernel(cs_ref, w_ref, o_ref):
    y = jnp.dot(cs_ref[...], w_ref[...], precision=lax.Precision.HIGHEST,
                preferred_element_type=F32)
    gd = w_ref.shape[0]
    o_ref[0, 0] = y[:gd].astype(BF)
    o_ref[1, 0] = y[gd:].astype(BF)


def _fnet_fold(w_out):
    d = w_out.shape[0]
    gd = d // FNET_GROUPS
    kk = (jnp.arange(gd, dtype=jnp.int32)[:, None] * jnp.arange(gd, dtype=jnp.int32)[None, :]) % gd
    ang = kk.astype(F32) * (2.0 * math.pi / gd)
    cs = jnp.concatenate([jnp.cos(ang), -jnp.sin(ang)], axis=0) * (gd ** -0.5)
    out = pl.pallas_call(
        _fnet_fold_kernel,
        grid=(FNET_GROUPS,),
        in_specs=[_resident((2 * gd, gd), lambda i: (0, 0)),
                  pl.BlockSpec((gd, d), lambda i: (i, 0))],
        out_specs=pl.BlockSpec((2, 1, gd, d), lambda i: (0, i, 0, 0)),
        out_shape=jax.ShapeDtypeStruct((2, FNET_GROUPS, gd, d), BF),
        compiler_params=_cparams(1, 32),
        name="fnet_fold",
    )(cs, w_out)
    return out.reshape(2 * d, d)


def _fnet1_kernel(x_ref, g_ref, t_ref, rep_ref, z_ref):
    n1, nsub, d = x_ref.shape
    rows = n1 * nsub
    h = _rms(x_ref[...].reshape(rows, d), g_ref[...]).astype(BF)
    r = lax.broadcasted_iota(jnp.int32, (rows, rows), 0)
    c = lax.broadcasted_iota(jnp.int32, (rows, rows), 1)
    same_n2 = (r % nsub) == (c % nsub)
    for ri in range(2):
        rep = _dot(rep_ref[...], t_ref[0, ri * n1:(ri + 1) * n1, :])
        m = jnp.where(same_n2, rep, 0.0).astype(BF)
        z_ref[ri] = _dot(m, h).reshape(n1, nsub, d)


def _fnet2_kernel(z_ref, x_ref, m3_ref, wf_ref, o_ref, ab_ref):
    n2, kb, d = x_ref.shape
    m3 = m3_ref[...]
    for j in range(kb):
        zz = jnp.concatenate([z_ref[0, j], z_ref[1, j]], axis=0).astype(BF)
        ab = _dot(m3, zz)
        ab_ref[j * n2:(j + 1) * n2, :d] = ab[:n2].astype(BF)
        ab_ref[j * n2:(j + 1) * n2, d:] = ab[n2:].astype(BF)
    y = _dot(ab_ref[...], wf_ref[...])
    for j in range(kb):
        o_ref[:, j, :] = x_ref[:, j, :] + y[j * n2:(j + 1) * n2]


def _fnet_tables(seq, nsub):
    n2 = FNET_N2
    n1 = seq // n2
    steps = n2 // nsub
    k1 = jnp.arange(n1, dtype=jnp.int32)
    nn = n2 * jnp.arange(n1, dtype=jnp.int32)[:, None] + jnp.arange(n2, dtype=jnp.int32)[None, :]
    ang1 = ((k1[:, None, None] * nn[None]) % seq).astype(F32) * (2.0 * math.pi / seq)
    tab = jnp.stack([jnp.cos(ang1), jnp.sin(ang1)], axis=0) * (n1 ** -0.5)
    m1 = tab.reshape(2, n1, n1, steps, nsub).transpose(3, 0, 1, 2, 4).reshape(
        steps, 2 * n1, n1 * nsub).astype(BF)
    rep = (jnp.arange(n1 * nsub, dtype=jnp.int32)[:, None] // nsub
           == jnp.arange(n1, dtype=jnp.int32)[None, :]).astype(BF)
    q = jnp.arange(n2, dtype=jnp.int32)
    ang3 = ((q[:, None] * q[None, :]) % n2).astype(F32) * (2.0 * math.pi / n2)
    c3, s3 = jnp.cos(ang3), jnp.sin(ang3)
    m3 = (jnp.concatenate([jnp.concatenate([c3, -s3], axis=1),
                           jnp.concatenate([s3, c3], axis=1)], axis=0) * (n2 ** -0.5)).astype(BF)
    return m1, rep, m3


def _fnet_group(x, g, wfold, *, row0, bsz, seq):
    t, d = x.shape
    n2 = FNET_N2
    n1 = seq // n2
    nsub = min(FNET_ROWS // n1, n2)
    kb = min(SUBLANES, n1)
    m1, rep, m3 = _fnet_tables(seq, nsub)
    rb = row0 // seq

    z = pl.pallas_call(
        _fnet1_kernel,
        grid=(bsz, n2 // nsub),
        in_specs=[
            pl.BlockSpec((n1, nsub, d), lambda b, j: (rb + b, j, 0)),
            _resident((1, d), lambda b, j: (0, 0)),
            pl.BlockSpec((1, 2 * n1, n1 * nsub), lambda b, j: (j, 0, 0)),
            _resident((n1 * nsub, n1), lambda b, j: (0, 0)),
        ],
        out_specs=pl.BlockSpec((None, 2, n1, nsub, d), lambda b, j: (b, 0, 0, j, 0)),
        out_shape=jax.ShapeDtypeStruct((bsz, 2, n1, n2, d), F32),
        compiler_params=_cparams(2, 56),
        name="fnet_stage1",
    )(x.reshape(t // n2, n2, d), g.reshape(1, d), m1, rep)

    out = pl.pallas_call(
        _fnet2_kernel,
        grid=(bsz, n1 // kb),
        in_specs=[
            pl.BlockSpec((None, 2, kb, n2, d), lambda b, j: (b, 0, j, 0, 0)),
            pl.BlockSpec((n2, kb, d), lambda b, j: (rb + b, j, 0)),
            _resident((2 * n2, 2 * n2), lambda b, j: (0, 0)),
            _resident((2 * d, d), lambda b, j: (0, 0)),
        ],
        out_specs=pl.BlockSpec((n2, kb, d), lambda b, j: (rb + b, j, 0)),
        out_shape=jax.ShapeDtypeStruct((t // n1, n1, d), F32),
        scratch_shapes=[pltpu.VMEM((kb * n2, 2 * d), BF)],
        input_output_aliases={1: 0},
        compiler_params=_cparams(2, 56),
        name="fnet_stage2",
    )(z, x.reshape(t // n1, n1, d), m3, wfold)
    return out.reshape(t, d)


def _sgu_kernel(x_ref, g_ref, win_ref, vg_ref, ws_ref, bs_ref, wout_ref, o_ref, v_ref, sv_ref, u_ref,
                *, tf):
    tm = x_ref.shape[0]
    half = vg_ref.shape[1]
    hd = half // GMLP_HEADS
    nc = half // tf
    x = x_ref[...]
    h = _rms(x, g_ref[...]).astype(BF)
    sqrt_half = math.sqrt(0.5)

    def gelu(a):
        return 0.5 * a * (1.0 + lax.erf(a * sqrt_half))

    ssq = jnp.zeros((tm, 1), F32)
    for c in range(nc):
        vc = gelu(_dot(h, win_ref[:, half + c * tf:half + (c + 1) * tf]))
        sv_ref[:, c * tf:(c + 1) * tf] = vc
        ssq = ssq + jnp.sum(vc * vc, axis=-1, keepdims=True)
    inv = lax.rsqrt(ssq * (1.0 / half) + EPS)
    for c in range(nc):
        cols = slice(c * tf, (c + 1) * tf)
        u_ref[:, cols] = gelu(_dot(h, win_ref[:, cols])).astype(BF)
        v_ref[:, cols] = (sv_ref[:, cols] * inv * vg_ref[:, cols]).astype(BF)
    nr = tm // CHUNK
    for hh in range(GMLP_HEADS):
        vv = jnp.concatenate([v_ref[r * CHUNK:(r + 1) * CHUNK, hh * hd:(hh + 1) * hd]
                              for r in range(nr)], axis=1)
        blk = _dot(ws_ref[hh], vv) + bs_ref[hh][:, 0:1]
        for r in range(nr):
            sv_ref[r * CHUNK:(r + 1) * CHUNK, hh * hd:(hh + 1) * hd] = blk[:, r * hd:(r + 1) * hd]
    acc = x
    for c in range(nc):
        cols = slice(c * tf, (c + 1) * tf)
        gated = (u_ref[:, cols].astype(F32) * sv_ref[:, cols]).astype(BF)
        acc = acc + _dot(gated, wout_ref[cols, :])
    o_ref[...] = acc


def _sgu_mixer(x, g, w_in, v_gain, w_s, b_s, w_out, *, tm, tf):
    t, d = x.shape
    half = v_gain.shape[0]
    bs = jnp.broadcast_to(b_s[:, :, None], (GMLP_HEADS, CHUNK, LANES)).astype(F32)
    return pl.pallas_call(
        functools.partial(_sgu_kernel, tf=tf),
        grid=(t // tm,),
        in_specs=[
            pl.BlockSpec((tm, d), lambda i: (i, 0)),
            _resident((1, d), lambda i: (0, 0)),
            _resident((d, 2 * half), lambda i: (0, 0)),
            _resident((1, half), lambda i: (0, 0)),
            _resident((GMLP_HEADS, CHUNK, CHUNK), lambda i: (0, 0, 0)),
            _resident((GMLP_HEADS, CHUNK, LANES), lambda i: (0, 0, 0)),
            _resident((half, d), lambda i: (0, 0)),
        ],
        out_specs=pl.BlockSpec((tm, d), lambda i: (i, 0)),
        out_shape=jax.ShapeDtypeStruct((t, d), F32),
        scratch_shapes=[pltpu.VMEM((tm, half), BF), pltpu.VMEM((tm, half), F32),
                        pltpu.VMEM((tm, half), BF)],
        compiler_params=_cparams(1, 56),
        name="sgu_mixer",
    )(x, g.reshape(1, d), w_in.astype(BF), v_gain.reshape(1, half), w_s.astype(BF), bs,
      w_out.astype(BF))


def _final_norm_kernel(x_ref, g_ref, o_ref):
    o_ref[...] = _rms(x_ref[...], g_ref[...])


def _final_norm(x, g, *, tm):
    t, d = x.shape
    return pl.pallas_call(
        _final_norm_kernel,
        grid=(t // tm,),
        in_specs=[pl.BlockSpec((tm, d), lambda i: (i, 0)), _resident((1, d), lambda i: (0, 0))],
        out_specs=pl.BlockSpec((tm, d), lambda i: (i, 0)),
        out_shape=jax.ShapeDtypeStruct((t, d), F32),
        compiler_params=_cparams(1, 32),
        name="final_norm",
    )(x, g.reshape(1, d))


def _tile(n, want):
    while n % want:
        want //= 2
    return want


def kernel(x_prompt, x_sample, norm_mix, norm_ffn, norm_final, conv_w_in, conv_w, conv_w_out, fnet_w_out, sgu_w_in, sgu_v_gain, sgu_w_s, sgu_b_s, sgu_w_out, ffn_w_gate, ffn_w_up, ffn_w_down, moe_w_router, moe_w_gate, moe_w_up, moe_w_down):
    bp, sp, d = x_prompt.shape
    bs_, ss, _ = x_sample.shape
    t_prompt = bp * sp
    t = t_prompt + bs_ * ss
    depth = norm_mix.shape[0]
    f = ffn_w_gate.shape[-1]
    tf = f // 7 if f % 7 == 0 else _tile(f, 512)
    tm = _tile(math.gcd(sp, ss), 512)
    half = sgu_v_gain.shape[-1]

    zero1 = jnp.zeros((t // tm,), jnp.int32)
    all_tiles = jnp.full((1,), t // tm, jnp.int32)
    ffn_w = [w.astype(BF)[:, None] for w in (ffn_w_gate, ffn_w_up, ffn_w_down)]
    moe_f32 = (moe_w_gate, moe_w_up, moe_w_down)
    moe_2d = [w.reshape(-1, w.shape[-1]) for w in moe_f32]

    x = None
    for i in range(depth):
        m, j = i % N_MIXERS, i // N_MIXERS
        if m == 0:
            src = (x, None) if i else (x_prompt.reshape(t_prompt, d), x_sample.reshape(t - t_prompt, d))
            x, cast_a = _conv_mixer(*src, norm_mix[i], conv_w_in[j], conv_w[j], conv_w_out[j], tm=tm,
                                    t_prompt=t_prompt, s_prompt=sp, s_sample=ss,
                                    cast=() if i else (moe_2d[0], moe_2d[2]))
        elif m == 1:
            wfold = _fnet_fold(fnet_w_out[j])
            x = _fnet_group(x, norm_mix[i], wfold, row0=0, bsz=bp, seq=sp)
            x = _fnet_group(x, norm_mix[i], wfold, row0=t_prompt, bsz=bs_, seq=ss)
        else:
            x = _sgu_mixer(x, norm_mix[i], sgu_w_in[j], sgu_v_gain[j], sgu_w_s[j], sgu_b_s[j],
                           sgu_w_out[j], tm=tm, tf=_tile(half, 512))
        k = i // 2
        last = i == depth - 1
        if i % 2 == 0:
            x, cast_b = _swiglu(x, norm_ffn[i], *ffn_w, k, zero1, all_tiles, tm=tm, tf=tf, dense=True,
                                cast=() if i else (moe_2d[1],))
            if i == 0:
                moe_w = [w2.reshape(w.shape) for w2, w in zip((cast_a[0], cast_b[0], cast_a[1]), moe_f32)]
            if last:
                x = _final_norm(x, norm_final, tm=tm)
                x = (x[:t_prompt], x[t_prompt:])
        else:
            x = _moe_layer(x, norm_ffn[i], moe_w_router[k], *moe_w, k, norm_final, tm=tm, tf=tf,
                           t_prompt=t_prompt if last else None)
    return (x[0].reshape(bp, sp, d), x[1].reshape(bs_, ss, d))
```

```python
import functools
import math

import jax
import jax.numpy as jnp
from jax import lax
from jax.experimental import pallas as pl
from jax.experimental.pallas import tpu as pltpu

EPS = 1e-6
BF = jnp.bfloat16
F32 = jnp.float32

N_MIXERS = 3
FNET_GROUPS = 8
GMLP_HEADS = 8
CHUNK = 128
TOP_K = 2
LANES = 128
SUBLANES = 8
HALO = 16
FNET_N2 = 128
FNET_ROWS = 1024


def _cparams(n_axes, vmem_mb):
    return pltpu.CompilerParams(
        dimension_semantics=("arbitrary",) * n_axes,
        vmem_limit_bytes=vmem_mb << 20,
    )


def _rms(x, g):
    ms = jnp.mean(x * x, axis=-1, keepdims=True)
    return x * lax.rsqrt(ms + EPS) * g


def _dot(a, b):
    return jnp.dot(a, b, preferred_element_type=F32)


def _resident(shape, index_map):
    return pl.BlockSpec(shape, index_map, pipeline_mode=pl.Buffered(1))


def _cast_specs(w, steps):
    rows, cols = w.shape
    nblk = 1
    while nblk * 2 <= steps and rows % (nblk * 2 * 16) == 0:
        nblk *= 2
    spec = pl.BlockSpec((rows // nblk, cols), lambda i, *_: (jnp.minimum(i, nblk - 1), 0))
    return spec, spec, jax.ShapeDtypeStruct((rows, cols), BF)


def _cast_blocks(cast_in, cast_out):
    for src, dst in zip(cast_in, cast_out):
        dst[...] = src[...].astype(BF)


def _conv_kernel(*refs, tm, t_prompt, s_prompt, s_sample, two_sources, n_cast):
    n_src = 6 if two_sources else 3
    g_ref, win_ref, cw_ref, wout_ref = refs[n_src:n_src + 4]
    cast_in = refs[n_src + 4:n_src + 4 + n_cast]
    o_ref = refs[n_src + 4 + n_cast]
    cast_out = refs[n_src + 5 + n_cast:n_src + 5 + 2 * n_cast]
    h_ref, cz_ref = refs[-2:]
    _cast_blocks(cast_in, cast_out)
    i = pl.program_id(0)
    d = o_ref.shape[1]
    g = g_ref[...]
    if two_sources:
        in_prompt = i * tm < t_prompt
        xp, xc, xn = (jnp.where(in_prompt, a[...], b[...]) for a, b in zip(refs[0:3], refs[3:6]))
    else:
        xp, xc, xn = (a[...] for a in refs[0:3])

    def seq_boundary(r):
        return jnp.where(r <= t_prompt, r % s_prompt == 0, (r - t_prompt) % s_sample == 0)

    first = seq_boundary(i * tm)
    last = seq_boundary(i * tm + tm)
    h_ref[0:HALO, :] = jnp.where(first, 0.0, _rms(xp, g)).astype(BF)
    h_ref[HALO:HALO + tm, :] = _rms(xc, g).astype(BF)
    h_ref[HALO + tm:, :] = jnp.where(last, 0.0, _rms(xn, g)).astype(BF)

    hext = h_ref[...]
    c = _dot(hext, win_ref[:, d:2 * d])
    z = _dot(hext, win_ref[:, 2 * d:])
    cz_ref[...] = c * z
    b = _dot(h_ref[HALO:HALO + tm, :], win_ref[:, :d])
    cw = cw_ref[...]
    conv = (cz_ref[HALO - 1:HALO - 1 + tm, :] * cw[0:1, :]
            + cz_ref[HALO:HALO + tm, :] * cw[1:2, :]
            + cz_ref[HALO + 1:HALO + 1 + tm, :] * cw[2:3, :])
    y = _dot((b * conv).astype(BF), wout_ref[...])
    o_ref[...] = xc + y


def _conv_mixer(x, x_sample, g, w_in, conv_w, w_out, *, tm, t_prompt, s_prompt, s_sample, cast=()):
    d = x.shape[1]
    xs = [x] if x_sample is None else [x, x_sample]
    t = sum(a.shape[0] for a in xs)
    nh = tm // HALO
    kern = functools.partial(_conv_kernel, tm=tm, t_prompt=t_prompt, s_prompt=s_prompt,
                             s_sample=s_sample, two_sources=len(xs) == 2, n_cast=len(cast))
    cast_specs = [_cast_specs(w, t // tm) for w in cast]

    def source_specs(rows, tile0):
        nt, nhb = rows // tm, rows // HALO
        cur = lambda i: jnp.clip(i - tile0, 0, nt - 1)
        return [
            pl.BlockSpec((HALO, d), lambda i: (jnp.maximum(cur(i) * nh - 1, 0), 0)),
            pl.BlockSpec((tm, d), lambda i: (cur(i), 0)),
            pl.BlockSpec((HALO, d), lambda i: (jnp.minimum((cur(i) + 1) * nh, nhb - 1), 0)),
        ]

    if len(xs) == 2:
        in_specs = source_specs(t_prompt, 0) + source_specs(t - t_prompt, t_prompt // tm)
        args = [xs[0]] * 3 + [xs[1]] * 3
    else:
        in_specs = source_specs(t, 0)
        args = [xs[0]] * 3
    in_specs += [
        _resident((1, d), lambda i: (0, 0)),
        _resident((d, 3 * d), lambda i: (0, 0)),
        _resident((3, d), lambda i: (0, 0)),
        _resident((d, d), lambda i: (0, 0)),
    ]
    args += [g.reshape(1, d), w_in.astype(BF), conv_w, w_out.astype(BF)]
    out = pl.pallas_call(
        kern,
        grid=(t // tm,),
        in_specs=in_specs + [s[0] for s in cast_specs],
        out_specs=[pl.BlockSpec((tm, d), lambda i: (i, 0))] + [s[1] for s in cast_specs],
        out_shape=[jax.ShapeDtypeStruct((t, d), F32)] + [s[2] for s in cast_specs],
        scratch_shapes=[pltpu.VMEM((tm + 2 * HALO, d), BF),
                        pltpu.VMEM((tm + 2 * HALO, d), F32)],
        compiler_params=_cparams(1, 56),
        name="conv_mixer",
    )(*args, *cast)
    return out[0], out[1:]


def _swiglu_kernel(te_ref, nu_ref, x_ref, g_ref, wg_ref, wu_ref, wd_ref, *rest, tf, dense):
    n_cast = len(rest) // 2
    o_ref = rest[n_cast]
    _cast_blocks(rest[:n_cast], rest[n_cast + 1:])
    i = pl.program_id(0)

    @pl.when(i < nu_ref[0])
    def _():
        x = x_ref[...]
        h = (_rms(x, g_ref[...]) if dense else x).astype(BF)
        f = wg_ref.shape[-1]
        acc = x if dense else jnp.zeros_like(x)
        for c in range(f // tf):
            a = _dot(h, wg_ref[:, c * tf:(c + 1) * tf])
            b = _dot(h, wu_ref[:, c * tf:(c + 1) * tf])
            act = (jax.nn.silu(a) * b).astype(BF)
            acc = acc + _dot(act, wd_ref[c * tf:(c + 1) * tf, :])
        o_ref[...] = acc

    @pl.when(i >= nu_ref[0])
    def _():
        o_ref[...] = jnp.zeros_like(o_ref)


def _swiglu(x, g, wg, wu, wd, layer, tile_expert, n_used, *, tm, tf, dense, cast=()):
    t, d = x.shape
    f = wg.shape[-1]
    kern = functools.partial(_swiglu_kernel, tf=tf, dense=dense)
    cast_specs = [_cast_specs(w, t // tm) for w in cast]
    prefetched = pl.BlockSpec if not dense else _resident
    grid_spec = pltpu.PrefetchScalarGridSpec(
        num_scalar_prefetch=2,
        grid=(t // tm,),
        in_specs=[
            pl.BlockSpec((tm, d), lambda i, te, nu: (i, 0)),
            _resident((1, d), lambda i, te, nu: (0, 0)),
            prefetched((None, None, d, f), lambda i, te, nu: (layer, te[i], 0, 0)),
            prefetched((None, None, d, f), lambda i, te, nu: (layer, te[i], 0, 0)),
            prefetched((None, None, f, d), lambda i, te, nu: (layer, te[i], 0, 0)),
        ] + [s[0] for s in cast_specs],
        out_specs=[pl.BlockSpec((tm, d), lambda i, te, nu: (i, 0))] + [s[1] for s in cast_specs],
    )
    out = pl.pallas_call(
        kern,
        grid_spec=grid_spec,
        out_shape=[jax.ShapeDtypeStruct((t, d), F32)] + [s[2] for s in cast_specs],
        compiler_params=_cparams(1, 56),
        name="swiglu_dense" if dense else "swiglu_routed",
    )(tile_expert, n_used, x, g.reshape(1, d), wg, wu, wd, *cast)
    return out[0], out[1:]


def _router_kernel(x_ref, g_ref, wr_ref, tri_ref, meta_ref, gate_ref, cnt_ref,
                   carry_ref, *, n_exp):
    i = pl.program_id(0)

    @pl.when(i == 0)
    def _():
        carry_ref[...] = jnp.zeros_like(carry_ref)

    h = _rms(x_ref[...], g_ref[...])
    h_hi = h.astype(BF)
    h_lo = (h - h_hi.astype(F32)).astype(BF)
    hw = _dot(h_hi, wr_ref[...])
    logits = hw[:, :LANES] + hw[:, LANES:] + _dot(h_lo, wr_ref[:, :LANES])
    lane = lax.broadcasted_iota(jnp.int32, logits.shape, 1).astype(F32)
    neg = jnp.float32(-jnp.inf)
    lg = jnp.where(lane < n_exp, logits, neg)
    v1 = jnp.max(lg, axis=1, keepdims=True)
    i1 = jnp.min(jnp.where(lg == v1, lane, float(LANES)), axis=1, keepdims=True)
    lg2 = jnp.where(lane == i1, neg, lg)
    v2 = jnp.max(lg2, axis=1, keepdims=True)
    i2 = jnp.min(jnp.where(lg2 == v2, lane, float(LANES)), axis=1, keepdims=True)
    e2 = jnp.exp(v2 - v1)
    g1 = 1.0 / (1.0 + e2)
    g2 = e2 / (1.0 + e2)

    sel = jnp.where((lane == i1) | (lane == i2), 1.0, 0.0)
    cum = _dot(tri_ref[...], sel.astype(BF)) + carry_ref[...]
    r1 = jnp.sum(jnp.where(lane == i1, cum, 0.0), axis=1, keepdims=True)
    r2 = jnp.sum(jnp.where(lane == i2, cum, 0.0), axis=1, keepdims=True)
    carry_ref[...] = carry_ref[...] + jnp.sum(sel, axis=0, keepdims=True)
    cnt_ref[...] = carry_ref[...]

    meta = jnp.where(lane == 0, i1, jnp.where(lane == 1, i2, jnp.where(
        lane == 2, r1, jnp.where(lane == 3, r2, 0.0))))
    meta_ref[...] = meta.T[0:SUBLANES, :]
    gate_ref[...] = jnp.where(lane == 0, g1, jnp.where(lane == 1, g2, 0.0))


def _router(x, g, w_router, *, tm):
    t, d = x.shape
    n_exp = w_router.shape[1]
    wr = jnp.zeros((d, LANES), F32).at[:, :n_exp].set(w_router.astype(F32))
    wr_hi = wr.astype(BF)
    wr = jnp.concatenate([wr_hi, (wr - wr_hi.astype(F32)).astype(BF)], axis=1)
    row = lax.broadcasted_iota(jnp.int32, (tm, tm), 0)
    col = lax.broadcasted_iota(jnp.int32, (tm, tm), 1)
    tri = (col < row).astype(BF)
    kern = functools.partial(_router_kernel, n_exp=n_exp)
    return pl.pallas_call(
        kern,
        grid=(t // tm,),
        in_specs=[
            pl.BlockSpec((tm, d), lambda i: (i, 0)),
            _resident((1, d), lambda i: (0, 0)),
            _resident((d, 2 * LANES), lambda i: (0, 0)),
            _resident((tm, tm), lambda i: (0, 0)),
        ],
        out_specs=[
            pl.BlockSpec((SUBLANES, tm), lambda i: (0, i)),
            pl.BlockSpec((tm, LANES), lambda i: (i, 0)),
            pl.BlockSpec((1, LANES), lambda i: (0, 0)),
        ],
        out_shape=[
            jax.ShapeDtypeStruct((SUBLANES, t), F32),
            jax.ShapeDtypeStruct((t, LANES), F32),
            jax.ShapeDtypeStruct((1, LANES), F32),
        ],
        scratch_shapes=[pltpu.VMEM((1, LANES), F32)],
        compiler_params=_cparams(1, 32),
        name="router",
    )(x, g.reshape(1, d), wr, tri)


def _dispatch_kernel(zs_ref, pos_ref, x_ref, g_ref, xs_ref, h_ref, zero_ref, sem, zsem, *, n_exp):
    tm = h_ref.shape[1]
    i = pl.program_id(0)

    @pl.when(i == 0)
    def _():
        zero_ref[...] = jnp.zeros_like(zero_ref)
        nz = zero_ref.shape[0]
        n_rows = xs_ref.shape[0]
        for e in range(n_exp):
            start = pl.multiple_of(zs_ref[e], SUBLANES)
            cp = pltpu.make_async_copy(zero_ref, xs_ref.at[pl.ds(start, nz), :], zsem)
            cp.start()
            cp.wait()
            cp = pltpu.make_async_copy(zero_ref.at[pl.ds(0, tm), :],
                                       xs_ref.at[pl.ds(n_rows - (e + 1) * tm, tm), :], zsem)
            cp.start()
            cp.wait()

    def scatter(s):
        h_ref[s] = _rms(x_ref[s * tm:(s + 1) * tm, :], g_ref[...])
        for r in range(tm):
            for k in range(TOP_K):
                p = pos_ref[s, 0, k * tm + r]
                pltpu.make_async_copy(h_ref.at[s, pl.ds(r, 1), :], xs_ref.at[pl.ds(p, 1), :],
                                      sem.at[s]).start(priority=k)

    def drain(s):
        for k in range(TOP_K):
            pltpu.make_async_copy(h_ref.at[s], xs_ref.at[pl.ds(0, tm), :], sem.at[s]).wait()

    scatter(0)

    @pl.when(i > 0)
    def _():
        drain(1)

    scatter(1)
    drain(0)

    @pl.when(i == pl.num_programs(0) - 1)
    def _():
        drain(1)


def _dispatch(x, g, pos3, zero_start, n_rows, *, tm):
    t, d = x.shape
    n_exp = zero_start.shape[0]
    grid_spec = pltpu.PrefetchScalarGridSpec(
        num_scalar_prefetch=1,
        grid=(t // (2 * tm),),
        in_specs=[
            pl.BlockSpec((2, 1, TOP_K * tm), lambda i, zs: (i, 0, 0), memory_space=pltpu.SMEM),
            pl.BlockSpec((2 * tm, d), lambda i, zs: (i, 0)),
            _resident((1, d), lambda i, zs: (0, 0)),
        ],
        out_specs=pl.BlockSpec(memory_space=pl.ANY),
        scratch_shapes=[pltpu.VMEM((2, tm, d), F32), pltpu.VMEM((tm + SUBLANES, d), F32),
                        pltpu.SemaphoreType.DMA((2,)), pltpu.SemaphoreType.DMA(())],
    )
    return pl.pallas_call(
        functools.partial(_dispatch_kernel, n_exp=n_exp),
        grid_spec=grid_spec,
        out_shape=jax.ShapeDtypeStruct((n_rows, d), F32),
        compiler_params=_cparams(1, 32),
        name="dispatch",
    )(zero_start, pos3, x, g.reshape(1, d))


def _gather_rows(pos_ref, tile, ys_ref, buf_ref, sem, slot, tm):
    for r in range(tm):
        for k in range(TOP_K):
            p = pos_ref[tile, 0, k * tm + r]
            pltpu.make_async_copy(ys_ref.at[pl.ds(p, 1), :], buf_ref.at[slot, k, pl.ds(r, 1), :],
                                  sem.at[slot]).start(priority=k)


def _combine_kernel(pos_ref, posn_ref, x_ref, gate_ref, gf_ref, ys_ref, *rest, n_prompt_steps):
    if n_prompt_steps is None:
        (o_ref,), (buf_ref, sem) = rest[:1], rest[1:]
    else:
        (op_ref, os_ref), (buf_ref, sem) = rest[:2], rest[2:]
    i = pl.program_id(0)
    n = pl.num_programs(0)
    tm = x_ref.shape[0] // 2

    def combine(slot):
        for k in range(TOP_K):
            pltpu.make_async_copy(ys_ref.at[pl.ds(0, tm), :], buf_ref.at[slot, k], sem.at[slot]).wait()
        rows = slice(slot * tm, (slot + 1) * tm)
        gate = gate_ref[rows, :]
        y = x_ref[rows, :] + gate[:, 0:1] * buf_ref[slot, 0] + gate[:, 1:2] * buf_ref[slot, 1]
        if n_prompt_steps is None:
            o_ref[rows, :] = y
        else:
            y = _rms(y, gf_ref[...])

            @pl.when(i < n_prompt_steps)
            def _():
                op_ref[rows, :] = y

            @pl.when(i >= n_prompt_steps)
            def _():
                os_ref[rows, :] = y

    @pl.when(i == 0)
    def _():
        _gather_rows(pos_ref, 0, ys_ref, buf_ref, sem, 0, tm)

    _gather_rows(pos_ref, 1, ys_ref, buf_ref, sem, 1, tm)
    combine(0)

    @pl.when(i + 1 < n)
    def _():
        _gather_rows(posn_ref, 0, ys_ref, buf_ref, sem, 0, tm)

    combine(1)


def _combine(x, gates, pos3, ys, g_final, *, tm, t_prompt=None):
    t, d = x.shape
    ns = t // (2 * tm)
    if t_prompt is None:
        nps = None
        out_specs = pl.BlockSpec((2 * tm, d), lambda i: (i, 0))
        out_shape = jax.ShapeDtypeStruct((t, d), F32)
    else:
        nps = t_prompt // (2 * tm)
        out_specs = [pl.BlockSpec((2 * tm, d), lambda i: (jnp.minimum(i, nps - 1), 0)),
                     pl.BlockSpec((2 * tm, d), lambda i: (jnp.maximum(i - nps, 0), 0))]
        out_shape = [jax.ShapeDtypeStruct((t_prompt, d), F32),
                     jax.ShapeDtypeStruct((t - t_prompt, d), F32)]
    return pl.pallas_call(
        functools.partial(_combine_kernel, n_prompt_steps=nps),
        grid=(ns,),
        in_specs=[
            pl.BlockSpec((2, 1, TOP_K * tm), lambda i: (i, 0, 0), memory_space=pltpu.SMEM),
            pl.BlockSpec((2, 1, TOP_K * tm), lambda i: (jnp.minimum(i + 1, ns - 1), 0, 0),
                         memory_space=pltpu.SMEM),
            pl.BlockSpec((2 * tm, d), lambda i: (i, 0)),
            pl.BlockSpec((2 * tm, LANES), lambda i: (i, 0)),
            _resident((1, d), lambda i: (0, 0)),
            pl.BlockSpec(memory_space=pl.ANY),
        ],
        out_specs=out_specs,
        out_shape=out_shape,
        scratch_shapes=[pltpu.VMEM((2, TOP_K, tm, d), F32), pltpu.SemaphoreType.DMA((2,))],
        compiler_params=_cparams(1, 48),
        name="combine",
    )(pos3, pos3, x, gates, g_final.reshape(1, d), ys)


def _moe_layer(x, g, w_router, wg, wu, wd, layer, g_final, *, tm, tf, t_prompt=None):
    t, d = x.shape
    n_exp = w_router.shape[1]
    meta, gates, cnt = _router(x, g, w_router, tm=tm)

    counts = cnt[0, :n_exp].astype(jnp.int32)
    padded = ((counts + tm - 1) // tm) * tm
    ends = jnp.cumsum(padded)
    base = ends - padded
    meta = meta.astype(jnp.int32)

    def positions(e, r):
        p = r
        for q in range(n_exp):
            p = p + jnp.where(e == q, base[q], 0)
        return p

    pos = jnp.stack([positions(meta[0], meta[2]), positions(meta[1], meta[3])], axis=0)
    pos3 = pos.reshape(TOP_K, t // tm, tm).transpose(1, 0, 2).reshape(t // tm, 1, TOP_K * tm)

    n_rows = TOP_K * t + n_exp * tm
    n_tiles = n_rows // tm
    tile_start = jnp.arange(n_tiles, dtype=jnp.int32) * tm
    tile_expert = jnp.minimum(
        jnp.sum((tile_start[:, None] >= ends[None, :]).astype(jnp.int32), axis=1), n_exp - 1)
    n_used = (ends[-1:] // tm).astype(jnp.int32)

    zero_start = (base + counts) // SUBLANES * SUBLANES
    xs = _dispatch(x, g, pos3, zero_start, n_rows, tm=tm)
    ys, _ = _swiglu(xs, g, wg, wu, wd, layer, tile_expert, n_used, tm=tm, tf=tf, dense=False)
    return _combine(x, gates, pos3, ys, g_final, tm=tm, t_prompt=t_prompt)


def _fnet_fold_kernel(cs_ref, w_ref, o_ref):
    y = jnp.dot(cs_ref[...], w_ref[...], precision=lax.Precision.HIGHEST,
                preferred_element_type=F32)
    gd = w_ref.shape[0]
    o_ref[0, 0] = y[:gd].astype(BF)
    o_ref[1, 0] = y[gd:].astype(BF)


def _fnet_fold(w_out):
    d = w_out.shape[0]
    gd = d // FNET_GROUPS
    kk = (jnp.arange(gd, dtype=jnp.int32)[:, None] * jnp.arange(gd, dtype=jnp.int32)[None, :]) % gd
    ang = kk.astype(F32) * (2.0 * math.pi / gd)
    cs = jnp.concatenate([jnp.cos(ang), -jnp.sin(ang)], axis=0) * (gd ** -0.5)
    out = pl.pallas_call(
        _fnet_fold_kernel,
        grid=(FNET_GROUPS,),
        in_specs=[_resident((2 * gd, gd), lambda i: (0, 0)),
                  pl.BlockSpec((gd, d), lambda i: (i, 0))],
        out_specs=pl.BlockSpec((2, 1, gd, d), lambda i: (0, i, 0, 0)),
        out_shape=jax.ShapeDtypeStruct((2, FNET_GROUPS, gd, d), BF),
        compiler_params=_cparams(1, 32),
        name="fnet_fold",
    )(cs, w_out)
    return out.reshape(2 * d, d)


def _fnet1_kernel(x_ref, g_ref, t_ref, rep_ref, z_ref):
    n1, nsub, d = x_ref.shape
    rows = n1 * nsub
    h = _rms(x_ref[...].reshape(rows, d), g_ref[...]).astype(BF)
    r = lax.broadcasted_iota(jnp.int32, (rows, rows), 0)
    c = lax.broadcasted_iota(jnp.int32, (rows, rows), 1)
    same_n2 = (r % nsub) == (c % nsub)
    for ri in range(2):
        rep = _dot(rep_ref[...], t_ref[0, ri * n1:(ri + 1) * n1, :])
        m = jnp.where(same_n2, rep, 0.0).astype(BF)
        z_ref[ri] = _dot(m, h).reshape(n1, nsub, d)


def _fnet2_kernel(z_ref, x_ref, m3_ref, wf_ref, o_ref, ab_ref):
    n2, kb, d = x_ref.shape
    m3 = m3_ref[...]
    for j in range(kb):
        zz = jnp.concatenate([z_ref[0, j], z_ref[1, j]], axis=0).astype(BF)
        ab = _dot(m3, zz)
        ab_ref[j * n2:(j + 1) * n2, :d] = ab[:n2].astype(BF)
        ab_ref[j * n2:(j + 1) * n2, d:] = ab[n2:].astype(BF)
    y = _dot(ab_ref[...], wf_ref[...])
    for j in range(kb):
        o_ref[:, j, :] = x_ref[:, j, :] + y[j * n2:(j + 1) * n2]


def _fnet_tables(seq, nsub):
    n2 = FNET_N2
    n1 = seq // n2
    steps = n2 // nsub
    k1 = jnp.arange(n1, dtype=jnp.int32)
    nn = n2 * jnp.arange(n1, dtype=jnp.int32)[:, None] + jnp.arange(n2, dtype=jnp.int32)[None, :]
    ang1 = ((k1[:, None, None] * nn[None]) % seq).astype(F32) * (2.0 * math.pi / seq)
    tab = jnp.stack([jnp.cos(ang1), jnp.sin(ang1)], axis=0) * (n1 ** -0.5)
    m1 = tab.reshape(2, n1, n1, steps, nsub).transpose(3, 0, 1, 2, 4).reshape(
        steps, 2 * n1, n1 * nsub).astype(BF)
    rep = (jnp.arange(n1 * nsub, dtype=jnp.int32)[:, None] // nsub
           == jnp.arange(n1, dtype=jnp.int32)[None, :]).astype(BF)
    q = jnp.arange(n2, dtype=jnp.int32)
    ang3 = ((q[:, None] * q[None, :]) % n2).astype(F32) * (2.0 * math.pi / n2)
    c3, s3 = jnp.cos(ang3), jnp.sin(ang3)
    m3 = (jnp.concatenate([jnp.concatenate([c3, -s3], axis=1),
                           jnp.concatenate([s3, c3], axis=1)], axis=0) * (n2 ** -0.5)).astype(BF)
    return m1, rep, m3


def _fnet_group(x, g, wfold, *, row0, bsz, seq):
    t, d = x.shape
    n2 = FNET_N2
    n1 = seq // n2
    nsub = min(FNET_ROWS // n1, n2)
    kb = min(SUBLANES, n1)
    m1, rep, m3 = _fnet_tables(seq, nsub)
    rb = row0 // seq

    z = pl.pallas_call(
        _fnet1_kernel,
        grid=(bsz, n2 // nsub),
        in_specs=[
            pl.BlockSpec((n1, nsub, d), lambda b, j: (rb + b, j, 0)),
            _resident((1, d), lambda b, j: (0, 0)),
            pl.BlockSpec((1, 2 * n1, n1 * nsub), lambda b, j: (j, 0, 0)),
            _resident((n1 * nsub, n1), lambda b, j: (0, 0)),
        ],
        out_specs=pl.BlockSpec((None, 2, n1, nsub, d), lambda b, j: (b, 0, 0, j, 0)),
        out_shape=jax.ShapeDtypeStruct((bsz, 2, n1, n2, d), F32),
        compiler_params=_cparams(2, 56),
        name="fnet_stage1",
    )(x.reshape(t // n2, n2, d), g.reshape(1, d), m1, rep)

    out = pl.pallas_call(
        _fnet2_kernel,
        grid=(bsz, n1 // kb),
        in_specs=[
            pl.BlockSpec((None, 2, kb, n2, d), lambda b, j: (b, 0, j, 0, 0)),
            pl.BlockSpec((n2, kb, d), lambda b, j: (rb + b, j, 0)),
            _resident((2 * n2, 2 * n2), lambda b, j: (0, 0)),
            _resident((2 * d, d), lambda b, j: (0, 0)),
        ],
        out_specs=pl.BlockSpec((n2, kb, d), lambda b, j: (rb + b, j, 0)),
        out_shape=jax.ShapeDtypeStruct((t // n1, n1, d), F32),
        scratch_shapes=[pltpu.VMEM((kb * n2, 2 * d), BF)],
        input_output_aliases={1: 0},
        compiler_params=_cparams(2, 56),
        name="fnet_stage2",
    )(z, x.reshape(t // n1, n1, d), m3, wfold)
    return out.reshape(t, d)


def _sgu_kernel(x_ref, g_ref, win_ref, vg_ref, ws_ref, bs_ref, wout_ref, o_ref, v_ref, sv_ref, u_ref,
                *, tf):
    tm = x_ref.shape[0]
    half = vg_ref.shape[1]
    hd = half // GMLP_HEADS
    nc = half // tf
    x = x_ref[...]
    h = _rms(x, g_ref[...]).astype(BF)
    sqrt_half = math.sqrt(0.5)

    def gelu(a):
        return 0.5 * a * (1.0 + lax.erf(a * sqrt_half))

    ssq = jnp.zeros((tm, 1), F32)
    for c in range(nc):
        vc = gelu(_dot(h, win_ref[:, half + c * tf:half + (c + 1) * tf]))
        sv_ref[:, c * tf:(c + 1) * tf] = vc
        ssq = ssq + jnp.sum(vc * vc, axis=-1, keepdims=True)
    inv = lax.rsqrt(ssq * (1.0 / half) + EPS)
    for c in range(nc):
        cols = slice(c * tf, (c + 1) * tf)
        u_ref[:, cols] = gelu(_dot(h, win_ref[:, cols])).astype(BF)
        v_ref[:, cols] = (sv_ref[:, cols] * inv * vg_ref[:, cols]).astype(BF)
    nr = tm // CHUNK
    for hh in range(GMLP_HEADS):
        vv = jnp.concatenate([v_ref[r * CHUNK:(r + 1) * CHUNK, hh * hd:(hh + 1) * hd]
                              for r in range(nr)], axis=1)
        blk = _dot(ws_ref[hh], vv) + bs_ref[hh][:, 0:1]
        for r in range(nr):
            sv_ref[r * CHUNK:(r + 1) * CHUNK, hh * hd:(hh + 1) * hd] = blk[:, r * hd:(r + 1) * hd]
    acc = x
    for c in range(nc):
        cols = slice(c * tf, (c + 1) * tf)
        gated = (u_ref[:, cols].astype(F32) * sv_ref[:, cols]).astype(BF)
        acc = acc + _dot(gated, wout_ref[cols, :])
    o_ref[...] = acc


def _sgu_mixer(x, g, w_in, v_gain, w_s, b_s, w_out, *, tm, tf):
    t, d = x.shape
    half = v_gain.shape[0]
    bs = jnp.broadcast_to(b_s[:, :, None], (GMLP_HEADS, CHUNK, LANES)).astype(F32)
    return pl.pallas_call(
        functools.partial(_sgu_kernel, tf=tf),
        grid=(t // tm,),
        in_specs=[
            pl.BlockSpec((tm, d), lambda i: (i, 0)),
            _resident((1, d), lambda i: (0, 0)),
            _resident((d, 2 * half), lambda i: (0, 0)),
            _resident((1, half), lambda i: (0, 0)),
            _resident((GMLP_HEADS, CHUNK, CHUNK), lambda i: (0, 0, 0)),
            _resident((GMLP_HEADS, CHUNK, LANES), lambda i: (0, 0, 0)),
            _resident((half, d), lambda i: (0, 0)),
        ],
        out_specs=pl.BlockSpec((tm, d), lambda i: (i, 0)),
        out_shape=jax.ShapeDtypeStruct((t, d), F32),
        scratch_shapes=[pltpu.VMEM((tm, half), BF), pltpu.VMEM((tm, half), F32),
                        pltpu.VMEM((tm, half), BF)],
        compiler_params=_cparams(1, 56),
        name="sgu_mixer",
    )(x, g.reshape(1, d), w_in.astype(BF), v_gain.reshape(1, half), w_s.astype(BF), bs,
      w_out.astype(BF))


def _final_norm_kernel(x_ref, g_ref, o_ref):
    o_ref[...] = _rms(x_ref[...], g_ref[...])


def _final_norm(x, g, *, tm):
    t, d = x.shape
    return pl.pallas_call(
        _final_norm_kernel,
        grid=(t // tm,),
        in_specs=[pl.BlockSpec((tm, d), lambda i: (i, 0)), _resident((1, d), lambda i: (0, 0))],
        out_specs=pl.BlockSpec((tm, d), lambda i: (i, 0)),
        out_shape=jax.ShapeDtypeStruct((t, d), F32),
        compiler_params=_cparams(1, 32),
        name="final_norm",
    )(x, g.reshape(1, d))


def _tile(n, want):
    while n % want:
        want //= 2
    return want


def kernel(x_prompt, x_sample, norm_mix, norm_ffn, norm_final, conv_w_in, conv_w, conv_w_out, fnet_w_out, sgu_w_in, sgu_v_gain, sgu_w_s, sgu_b_s, sgu_w_out, ffn_w_gate, ffn_w_up, ffn_w_down, moe_w_router, moe_w_gate, moe_w_up, moe_w_down):
    bp, sp, d = x_prompt.shape
    bs_, ss, _ = x_sample.shape
    t_prompt = bp * sp
    t = t_prompt + bs_ * ss
    depth = norm_mix.shape[0]
    f = ffn_w_gate.shape[-1]
    tf = next(c for c in range(512, 0, -LANES) if f % c == 0)
    tm = _tile(math.gcd(sp, ss), 512)
    half = sgu_v_gain.shape[-1]

    zero1 = jnp.zeros((t // tm,), jnp.int32)
    all_tiles = jnp.full((1,), t // tm, jnp.int32)
    to_2d = lambda w: w.reshape(-1, w.shape[-1])
    conv_riders = (moe_w_gate, moe_w_down, ffn_w_gate, ffn_w_up, ffn_w_down, sgu_w_in, sgu_w_out)

    x = None
    for i in range(depth):
        m, j = i % N_MIXERS, i // N_MIXERS
        if m == 0:
            src = (x, None) if i else (x_prompt.reshape(t_prompt, d), x_sample.reshape(t - t_prompt, d))
            x, cast_a = _conv_mixer(*src, norm_mix[i], conv_w_in[j], conv_w[j], conv_w_out[j], tm=tm,
                                    t_prompt=t_prompt, s_prompt=sp, s_sample=ss,
                                    cast=() if i else [to_2d(w) for w in conv_riders])
            if i == 0:
                moe_g, moe_d, ffn_g, ffn_u, ffn_d, sgu_w_in, sgu_w_out = (
                    c.reshape(w.shape) for c, w in zip(cast_a, conv_riders))
                ffn_w = [w[:, None] for w in (ffn_g, ffn_u, ffn_d)]
        elif m == 1:
            wfold = _fnet_fold(fnet_w_out[j])
            x = _fnet_group(x, norm_mix[i], wfold, row0=0, bsz=bp, seq=sp)
            x = _fnet_group(x, norm_mix[i], wfold, row0=t_prompt, bsz=bs_, seq=ss)
        else:
            x = _sgu_mixer(x, norm_mix[i], sgu_w_in[j], sgu_v_gain[j], sgu_w_s[j], sgu_b_s[j],
                           sgu_w_out[j], tm=tm, tf=_tile(half, 512))
        k = i // 2
        last = i == depth - 1
        if i % 2 == 0:
            x, cast_b = _swiglu(x, norm_ffn[i], *ffn_w, k, zero1, all_tiles, tm=tm, tf=tf, dense=True,
                                cast=() if i else (to_2d(moe_w_up),))
            if i == 0:
                moe_w = [moe_g, cast_b[0].reshape(moe_w_up.shape), moe_d]
            if last:
                x = _final_norm(x, norm_final, tm=tm)
                x = (x[:t_prompt], x[t_prompt:])
        else:
            x = _moe_layer(x, norm_ffn[i], moe_w_router[k], *moe_w, k, norm_final, tm=tm, tf=tf,
                           t_prompt=t_prompt if last else None)
    return (x[0].reshape(bp, sp, d), x[1].reshape(bs_, ss, d))
```

```python
import functools
import math

import jax
import jax.numpy as jnp
from jax import lax
from jax.experimental import pallas as pl
from jax.experimental.pallas import tpu as pltpu

EPS = 1e-6
BF = jnp.bfloat16
F32 = jnp.float32

N_MIXERS = 3
FNET_GROUPS = 8
GMLP_HEADS = 8
CHUNK = 128
TOP_K = 2
LANES = 128
SUBLANES = 8
HALO = 16
FNET_N2 = 128
FNET_ROWS = 1024


def _cparams(n_axes, vmem_mb):
    return pltpu.CompilerParams(
        dimension_semantics=("arbitrary",) * n_axes,
        vmem_limit_bytes=vmem_mb << 20,
    )


def _rms(x, g):
    ms = jnp.mean(x * x, axis=-1, keepdims=True)
    return x * lax.rsqrt(ms + EPS) * g


def _dot(a, b):
    return jnp.dot(a, b, preferred_element_type=F32)


def _resident(shape, index_map):
    return pl.BlockSpec(shape, index_map, pipeline_mode=pl.Buffered(1))


def _cast_specs(w, steps):
    rows, cols = w.shape
    nblk = 1
    while nblk * 2 <= steps and rows % (nblk * 2 * 16) == 0:
        nblk *= 2
    spec = pl.BlockSpec((rows // nblk, cols), lambda i, *_: (jnp.minimum(i, nblk - 1), 0))
    return spec, spec, jax.ShapeDtypeStruct((rows, cols), BF)


def _cast_blocks(cast_in, cast_out):
    for src, dst in zip(cast_in, cast_out):
        dst[...] = src[...].astype(BF)


def _conv_kernel(*refs, tm, t_prompt, s_prompt, s_sample, two_sources, n_cast):
    n_src = 6 if two_sources else 3
    g_ref, win_ref, cw_ref, wout_ref = refs[n_src:n_src + 4]
    cast_in = refs[n_src + 4:n_src + 4 + n_cast]
    o_ref = refs[n_src + 4 + n_cast]
    cast_out = refs[n_src + 5 + n_cast:n_src + 5 + 2 * n_cast]
    h_ref, cz_ref = refs[-2:]
    _cast_blocks(cast_in, cast_out)
    i = pl.program_id(0)
    d = o_ref.shape[1]
    g = g_ref[...]
    if two_sources:
        in_prompt = i * tm < t_prompt
        xp, xc, xn = (jnp.where(in_prompt, a[...], b[...]) for a, b in zip(refs[0:3], refs[3:6]))
    else:
        xp, xc, xn = (a[...] for a in refs[0:3])

    def seq_boundary(r):
        return jnp.where(r <= t_prompt, r % s_prompt == 0, (r - t_prompt) % s_sample == 0)

    first = seq_boundary(i * tm)
    last = seq_boundary(i * tm + tm)
    h_ref[0:HALO, :] = jnp.where(first, 0.0, _rms(xp, g)).astype(BF)
    h_ref[HALO:HALO + tm, :] = _rms(xc, g).astype(BF)
    h_ref[HALO + tm:, :] = jnp.where(last, 0.0, _rms(xn, g)).astype(BF)

    hext = h_ref[...]
    c = _dot(hext, win_ref[:, d:2 * d])
    z = _dot(hext, win_ref[:, 2 * d:])
    cz_ref[...] = c * z
    b = _dot(h_ref[HALO:HALO + tm, :], win_ref[:, :d])
    cw = cw_ref[...]
    conv = (cz_ref[HALO - 1:HALO - 1 + tm, :] * cw[0:1, :]
            + cz_ref[HALO:HALO + tm, :] * cw[1:2, :]
            + cz_ref[HALO + 1:HALO + 1 + tm, :] * cw[2:3, :])
    y = _dot((b * conv).astype(BF), wout_ref[...])
    o_ref[...] = xc + y


def _conv_mixer(x, x_sample, g, w_in, conv_w, w_out, *, tm, t_prompt, s_prompt, s_sample, cast=()):
    d = x.shape[1]
    xs = [x] if x_sample is None else [x, x_sample]
    t = sum(a.shape[0] for a in xs)
    nh = tm // HALO
    kern = functools.partial(_conv_kernel, tm=tm, t_prompt=t_prompt, s_prompt=s_prompt,
                             s_sample=s_sample, two_sources=len(xs) == 2, n_cast=len(cast))
    cast_specs = [_cast_specs(w, t // tm) for w in cast]

    def source_specs(rows, tile0):
        nt, nhb = rows // tm, rows // HALO
        cur = lambda i: jnp.clip(i - tile0, 0, nt - 1)
        return [
            pl.BlockSpec((HALO, d), lambda i: (jnp.maximum(cur(i) * nh - 1, 0), 0)),
            pl.BlockSpec((tm, d), lambda i: (cur(i), 0)),
            pl.BlockSpec((HALO, d), lambda i: (jnp.minimum((cur(i) + 1) * nh, nhb - 1), 0)),
        ]

    if len(xs) == 2:
        in_specs = source_specs(t_prompt, 0) + source_specs(t - t_prompt, t_prompt // tm)
        args = [xs[0]] * 3 + [xs[1]] * 3
    else:
        in_specs = source_specs(t, 0)
        args = [xs[0]] * 3
    in_specs += [
        _resident((1, d), lambda i: (0, 0)),
        _resident((d, 3 * d), lambda i: (0, 0)),
        _resident((3, d), lambda i: (0, 0)),
        _resident((d, d), lambda i: (0, 0)),
    ]
    args += [g.reshape(1, d), w_in.astype(BF), conv_w, w_out.astype(BF)]
    out = pl.pallas_call(
        kern,
        grid=(t // tm,),
        in_specs=in_specs + [s[0] for s in cast_specs],
        out_specs=[pl.BlockSpec((tm, d), lambda i: (i, 0))] + [s[1] for s in cast_specs],
        out_shape=[jax.ShapeDtypeStruct((t, d), F32)] + [s[2] for s in cast_specs],
        scratch_shapes=[pltpu.VMEM((tm + 2 * HALO, d), BF),
                        pltpu.VMEM((tm + 2 * HALO, d), F32)],
        compiler_params=_cparams(1, 56),
        name="conv_mixer",
    )(*args, *cast)
    return out[0], out[1:]


def _swiglu_kernel(te_ref, nu_ref, x_ref, g_ref, wg_ref, wu_ref, wd_ref, *rest, tf, dense):
    n_cast = len(rest) // 2
    o_ref = rest[n_cast]
    _cast_blocks(rest[:n_cast], rest[n_cast + 1:])
    i = pl.program_id(0)

    @pl.when(i < nu_ref[0])
    def _():
        x = x_ref[...]
        h = (_rms(x, g_ref[...]) if dense else x).astype(BF)
        f = wg_ref.shape[-1]
        acc = x if dense else jnp.zeros_like(x)
        for c in range(f // tf):
            a = _dot(h, wg_ref[:, c * tf:(c + 1) * tf])
            b = _dot(h, wu_ref[:, c * tf:(c + 1) * tf])
            act = (jax.nn.silu(a) * b).astype(BF)
            acc = acc + _dot(act, wd_ref[c * tf:(c + 1) * tf, :])
        o_ref[...] = acc

    @pl.when(i >= nu_ref[0])
    def _():
        o_ref[...] = jnp.zeros_like(o_ref)


def _swiglu(x, g, wg, wu, wd, layer, tile_expert, n_used, *, tm, tf, dense, cast=()):
    t, d = x.shape
    f = wg.shape[-1]
    kern = functools.partial(_swiglu_kernel, tf=tf, dense=dense)
    cast_specs = [_cast_specs(w, t // tm) for w in cast]
    prefetched = pl.BlockSpec if not dense else _resident
    grid_spec = pltpu.PrefetchScalarGridSpec(
        num_scalar_prefetch=2,
        grid=(t // tm,),
        in_specs=[
            pl.BlockSpec((tm, d), lambda i, te, nu: (i, 0)),
            _resident((1, d), lambda i, te, nu: (0, 0)),
            prefetched((None, None, d, f), lambda i, te, nu: (layer, te[i], 0, 0)),
            prefetched((None, None, d, f), lambda i, te, nu: (layer, te[i], 0, 0)),
            prefetched((None, None, f, d), lambda i, te, nu: (layer, te[i], 0, 0)),
        ] + [s[0] for s in cast_specs],
        out_specs=[pl.BlockSpec((tm, d), lambda i, te, nu: (i, 0))] + [s[1] for s in cast_specs],
    )
    out = pl.pallas_call(
        kern,
        grid_spec=grid_spec,
        out_shape=[jax.ShapeDtypeStruct((t, d), F32)] + [s[2] for s in cast_specs],
        compiler_params=_cparams(1, 56),
        name="swiglu_dense" if dense else "swiglu_routed",
    )(tile_expert, n_used, x, g.reshape(1, d), wg, wu, wd, *cast)
    return out[0], out[1:]


def _router_kernel(x_ref, g_ref, wr_ref, tri_ref, meta_ref, gate_ref, cnt_ref,
                   carry_ref, *, n_exp):
    i = pl.program_id(0)

    @pl.when(i == 0)
    def _():
        carry_ref[...] = jnp.zeros_like(carry_ref)

    h = _rms(x_ref[...], g_ref[...])
    h_hi = h.astype(BF)
    h_lo = (h - h_hi.astype(F32)).astype(BF)
    hw = _dot(h_hi, wr_ref[...])
    logits = hw[:, :LANES] + hw[:, LANES:] + _dot(h_lo, wr_ref[:, :LANES])
    lane = lax.broadcasted_iota(jnp.int32, logits.shape, 1).astype(F32)
    neg = jnp.float32(-jnp.inf)
    lg = jnp.where(lane < n_exp, logits, neg)
    v1 = jnp.max(lg, axis=1, keepdims=True)
    i1 = jnp.min(jnp.where(lg == v1, lane, float(LANES)), axis=1, keepdims=True)
    lg2 = jnp.where(lane == i1, neg, lg)
    v2 = jnp.max(lg2, axis=1, keepdims=True)
    i2 = jnp.min(jnp.where(lg2 == v2, lane, float(LANES)), axis=1, keepdims=True)
    e2 = jnp.exp(v2 - v1)
    g1 = 1.0 / (1.0 + e2)
    g2 = e2 / (1.0 + e2)

    sel = jnp.where((lane == i1) | (lane == i2), 1.0, 0.0)
    cum = _dot(tri_ref[...], sel.astype(BF)) + carry_ref[...]
    r1 = jnp.sum(jnp.where(lane == i1, cum, 0.0), axis=1, keepdims=True)
    r2 = jnp.sum(jnp.where(lane == i2, cum, 0.0), axis=1, keepdims=True)
    carry_ref[...] = carry_ref[...] + jnp.sum(sel, axis=0, keepdims=True)
    cnt_ref[...] = carry_ref[...]

    meta = jnp.where(lane == 0, i1, jnp.where(lane == 1, i2, jnp.where(
        lane == 2, r1, jnp.where(lane == 3, r2, 0.0))))
    meta_ref[...] = meta.T[0:SUBLANES, :]
    gate_ref[...] = jnp.where(lane == 0, g1, jnp.where(lane == 1, g2, 0.0))


def _router(x, g, w_router, *, tm):
    t, d = x.shape
    n_exp = w_router.shape[1]
    wr = jnp.zeros((d, LANES), F32).at[:, :n_exp].set(w_router.astype(F32))
    wr_hi = wr.astype(BF)
    wr = jnp.concatenate([wr_hi, (wr - wr_hi.astype(F32)).astype(BF)], axis=1)
    row = lax.broadcasted_iota(jnp.int32, (tm, tm), 0)
    col = lax.broadcasted_iota(jnp.int32, (tm, tm), 1)
    tri = (col < row).astype(BF)
    kern = functools.partial(_router_kernel, n_exp=n_exp)
    return pl.pallas_call(
        kern,
        grid=(t // tm,),
        in_specs=[
            pl.BlockSpec((tm, d), lambda i: (i, 0)),
            _resident((1, d), lambda i: (0, 0)),
            _resident((d, 2 * LANES), lambda i: (0, 0)),
            _resident((tm, tm), lambda i: (0, 0)),
        ],
        out_specs=[
            pl.BlockSpec((SUBLANES, tm), lambda i: (0, i)),
            pl.BlockSpec((tm, LANES), lambda i: (i, 0)),
            pl.BlockSpec((1, LANES), lambda i: (0, 0)),
        ],
        out_shape=[
            jax.ShapeDtypeStruct((SUBLANES, t), F32),
            jax.ShapeDtypeStruct((t, LANES), F32),
            jax.ShapeDtypeStruct((1, LANES), F32),
        ],
        scratch_shapes=[pltpu.VMEM((1, LANES), F32)],
        compiler_params=_cparams(1, 32),
        name="router",
    )(x, g.reshape(1, d), wr, tri)


def _dispatch_kernel(zs_ref, pos_ref, x_ref, g_ref, xs_ref, h_ref, zero_ref, sem, zsem, *, n_exp):
    tm = h_ref.shape[1]
    i = pl.program_id(0)

    @pl.when(i == 0)
    def _():
        zero_ref[...] = jnp.zeros_like(zero_ref)
        nz = zero_ref.shape[0]
        n_rows = xs_ref.shape[0]
        for e in range(n_exp):
            start = pl.multiple_of(zs_ref[e], SUBLANES)
            cp = pltpu.make_async_copy(zero_ref, xs_ref.at[pl.ds(start, nz), :], zsem)
            cp.start()
            cp.wait()
            cp = pltpu.make_async_copy(zero_ref.at[pl.ds(0, tm), :],
                                       xs_ref.at[pl.ds(n_rows - (e + 1) * tm, tm), :], zsem)
            cp.start()
            cp.wait()

    def scatter(s):
        h_ref[s] = _rms(x_ref[s * tm:(s + 1) * tm, :], g_ref[...])
        for r in range(tm):
            for k in range(TOP_K):
                p = pos_ref[s, 0, k * tm + r]
                pltpu.make_async_copy(h_ref.at[s, pl.ds(r, 1), :], xs_ref.at[pl.ds(p, 1), :],
                                      sem.at[s]).start(priority=k)

    def drain(s):
        for k in range(TOP_K):
            pltpu.make_async_copy(h_ref.at[s], xs_ref.at[pl.ds(0, tm), :], sem.at[s]).wait()

    scatter(0)

    @pl.when(i > 0)
    def _():
        drain(1)

    scatter(1)
    drain(0)

    @pl.when(i == pl.num_programs(0) - 1)
    def _():
        drain(1)


def _dispatch(x, g, pos3, zero_start, n_rows, *, tm):
    t, d = x.shape
    n_exp = zero_start.shape[0]
    grid_spec = pltpu.PrefetchScalarGridSpec(
        num_scalar_prefetch=1,
        grid=(t // (2 * tm),),
        in_specs=[
            pl.BlockSpec((2, 1, TOP_K * tm), lambda i, zs: (i, 0, 0), memory_space=pltpu.SMEM),
            pl.BlockSpec((2 * tm, d), lambda i, zs: (i, 0)),
            _resident((1, d), lambda i, zs: (0, 0)),
        ],
        out_specs=pl.BlockSpec(memory_space=pl.ANY),
        scratch_shapes=[pltpu.VMEM((2, tm, d), F32), pltpu.VMEM((tm + SUBLANES, d), F32),
                        pltpu.SemaphoreType.DMA((2,)), pltpu.SemaphoreType.DMA(())],
    )
    return pl.pallas_call(
        functools.partial(_dispatch_kernel, n_exp=n_exp),
        grid_spec=grid_spec,
        out_shape=jax.ShapeDtypeStruct((n_rows, d), F32),
        compiler_params=_cparams(1, 32),
        name="dispatch",
    )(zero_start, pos3, x, g.reshape(1, d))


def _gather_rows(pos_ref, tile, ys_ref, buf_ref, sem, slot, tm):
    for r in range(tm):
        for k in range(TOP_K):
            p = pos_ref[tile, 0, k * tm + r]
            pltpu.make_async_copy(ys_ref.at[pl.ds(p, 1), :], buf_ref.at[slot, k, pl.ds(r, 1), :],
                                  sem.at[slot]).start(priority=k)


def _combine_kernel(pos_ref, posn_ref, x_ref, gate_ref, gf_ref, ys_ref, *rest, n_prompt_steps):
    if n_prompt_steps is None:
        (o_ref,), (buf_ref, sem) = rest[:1], rest[1:]
    else:
        (op_ref, os_ref), (buf_ref, sem) = rest[:2], rest[2:]
    i = pl.program_id(0)
    n = pl.num_programs(0)
    tm = x_ref.shape[0] // 2

    def combine(slot):
        for k in range(TOP_K):
            pltpu.make_async_copy(ys_ref.at[pl.ds(0, tm), :], buf_ref.at[slot, k], sem.at[slot]).wait()
        rows = slice(slot * tm, (slot + 1) * tm)
        gate = gate_ref[rows, :]
        y = x_ref[rows, :] + gate[:, 0:1] * buf_ref[slot, 0] + gate[:, 1:2] * buf_ref[slot, 1]
        if n_prompt_steps is None:
            o_ref[rows, :] = y
        else:
            y = _rms(y, gf_ref[...])

            @pl.when(i < n_prompt_steps)
            def _():
                op_ref[rows, :] = y

            @pl.when(i >= n_prompt_steps)
            def _():
                os_ref[rows, :] = y

    @pl.when(i == 0)
    def _():
        _gather_rows(pos_ref, 0, ys_ref, buf_ref, sem, 0, tm)

    _gather_rows(pos_ref, 1, ys_ref, buf_ref, sem, 1, tm)
    combine(0)

    @pl.when(i + 1 < n)
    def _():
        _gather_rows(posn_ref, 0, ys_ref, buf_ref, sem, 0, tm)

    combine(1)


def _combine(x, gates, pos3, ys, g_final, *, tm, t_prompt=None):
    t, d = x.shape
    ns = t // (2 * tm)
    if t_prompt is None:
        nps = None
        out_specs = pl.BlockSpec((2 * tm, d), lambda i: (i, 0))
        out_shape = jax.ShapeDtypeStruct((t, d), F32)
    else:
        nps = t_prompt // (2 * tm)
        out_specs = [pl.BlockSpec((2 * tm, d), lambda i: (jnp.minimum(i, nps - 1), 0)),
                     pl.BlockSpec((2 * tm, d), lambda i: (jnp.maximum(i - nps, 0), 0))]
        out_shape = [jax.ShapeDtypeStruct((t_prompt, d), F32),
                     jax.ShapeDtypeStruct((t - t_prompt, d), F32)]
    return pl.pallas_call(
        functools.partial(_combine_kernel, n_prompt_steps=nps),
        grid=(ns,),
        in_specs=[
            pl.BlockSpec((2, 1, TOP_K * tm), lambda i: (i, 0, 0), memory_space=pltpu.SMEM),
            pl.BlockSpec((2, 1, TOP_K * tm), lambda i: (jnp.minimum(i + 1, ns - 1), 0, 0),
                         memory_space=pltpu.SMEM),
            pl.BlockSpec((2 * tm, d), lambda i: (i, 0)),
            pl.BlockSpec((2 * tm, LANES), lambda i: (i, 0)),
            _resident((1, d), lambda i: (0, 0)),
            pl.BlockSpec(memory_space=pl.ANY),
        ],
        out_specs=out_specs,
        out_shape=out_shape,
        scratch_shapes=[pltpu.VMEM((2, TOP_K, tm, d), F32), pltpu.SemaphoreType.DMA((2,))],
        compiler_params=_cparams(1, 48),
        name="combine",
    )(pos3, pos3, x, gates, g_final.reshape(1, d), ys)


def _moe_layer(x, g, w_router, wg, wu, wd, layer, g_final, *, tm, tf, t_prompt=None):
    t, d = x.shape
    n_exp = w_router.shape[1]
    meta, gates, cnt = _router(x, g, w_router, tm=tm)

    counts = cnt[0, :n_exp].astype(jnp.int32)
    padded = ((counts + tm - 1) // tm) * tm
    ends = jnp.cumsum(padded)
    base = ends - padded
    meta = meta.astype(jnp.int32)

    def positions(e, r):
        p = r
        for q in range(n_exp):
            p = p + jnp.where(e == q, base[q], 0)
        return p

    pos = jnp.stack([positions(meta[0], meta[2]), positions(meta[1], meta[3])], axis=0)
    pos3 = pos.reshape(TOP_K, t // tm, tm).transpose(1, 0, 2).reshape(t // tm, 1, TOP_K * tm)

    n_rows = TOP_K * t + n_exp * tm
    n_tiles = n_rows // tm
    tile_start = jnp.arange(n_tiles, dtype=jnp.int32) * tm
    tile_expert = jnp.minimum(
        jnp.sum((tile_start[:, None] >= ends[None, :]).astype(jnp.int32), axis=1), n_exp - 1)
    n_used = (ends[-1:] // tm).astype(jnp.int32)

    zero_start = (base + counts) // SUBLANES * SUBLANES
    xs = _dispatch(x, g, pos3, zero_start, n_rows, tm=tm)
    ys, _ = _swiglu(xs, g, wg, wu, wd, layer, tile_expert, n_used, tm=tm, tf=tf, dense=False)
    return _combine(x, gates, pos3, ys, g_final, tm=tm, t_prompt=t_prompt)


def _fnet_fold_kernel(cs_ref, w_ref, o_ref):
    y = jnp.dot(cs_ref[...], w_ref[...], precision=lax.Precision.HIGHEST,
                preferred_element_type=F32)
    gd = w_ref.shape[0]
    o_ref[0, 0] = y[:gd].astype(BF)
    o_ref[1, 0] = y[gd:].astype(BF)


def _fnet_fold(w_out):
    d = w_out.shape[0]
    gd = d // FNET_GROUPS
    kk = (jnp.arange(gd, dtype=jnp.int32)[:, None] * jnp.arange(gd, dtype=jnp.int32)[None, :]) % gd
    ang = kk.astype(F32) * (2.0 * math.pi / gd)
    cs = jnp.concatenate([jnp.cos(ang), -jnp.sin(ang)], axis=0) * (gd ** -0.5)
    out = pl.pallas_call(
        _fnet_fold_kernel,
        grid=(FNET_GROUPS,),
        in_specs=[_resident((2 * gd, gd), lambda i: (0, 0)),
                  pl.BlockSpec((gd, d), lambda i: (i, 0))],
        out_specs=pl.BlockSpec((2, 1, gd, d), lambda i: (0, i, 0, 0)),
        out_shape=jax.ShapeDtypeStruct((2, FNET_GROUPS, gd, d), BF),
        compiler_params=_cparams(1, 32),
        name="fnet_fold",
    )(cs, w_out)
    return out.reshape(2 * d, d)


def _fnet1_kernel(x_ref, g_ref, t_ref, rep_ref, z_ref):
    n1, nsub, d = x_ref.shape
    rows = n1 * nsub
    h = _rms(x_ref[...].reshape(rows, d), g_ref[...]).astype(BF)
    r = lax.broadcasted_iota(jnp.int32, (rows, rows), 0)
    c = lax.broadcasted_iota(jnp.int32, (rows, rows), 1)
    same_n2 = (r % nsub) == (c % nsub)
    for ri in range(2):
        rep = _dot(rep_ref[...], t_ref[0, ri * n1:(ri + 1) * n1, :])
        m = jnp.where(same_n2, rep, 0.0).astype(BF)
        z_ref[ri] = _dot(m, h).reshape(n1, nsub, d)


def _fnet2_kernel(z_ref, x_ref, m3_ref, wf_ref, o_ref, ab_ref):
    n2, kb, d = x_ref.shape
    m3 = m3_ref[...]
    for j in range(kb):
        zz = jnp.concatenate([z_ref[0, j], z_ref[1, j]], axis=0).astype(BF)
        ab = _dot(m3, zz)
        ab_ref[j * n2:(j + 1) * n2, :d] = ab[:n2].astype(BF)
        ab_ref[j * n2:(j + 1) * n2, d:] = ab[n2:].astype(BF)
    y = _dot(ab_ref[...], wf_ref[...])
    for j in range(kb):
        o_ref[:, j, :] = x_ref[:, j, :] + y[j * n2:(j + 1) * n2]


def _fnet_tables(seq, nsub):
    n2 = FNET_N2
    n1 = seq // n2
    steps = n2 // nsub
    k1 = jnp.arange(n1, dtype=jnp.int32)
    ang_a = ((k1[:, None] * k1[None, :]) % n1).astype(F32) * (2.0 * math.pi / n1)
    ang_b = ((k1[:, None] * jnp.arange(n2, dtype=jnp.int32)[None, :]) % seq).astype(F32) * (
        2.0 * math.pi / seq)
    ca, sa = (f(ang_a)[None, :, :, None] for f in (jnp.cos, jnp.sin))
    cb, sb = (f(ang_b).reshape(n1, steps, nsub).transpose(1, 0, 2)[:, :, None, :]
              for f in (jnp.cos, jnp.sin))
    tab = jnp.stack([ca * cb - sa * sb, sa * cb + ca * sb], axis=1) * (n1 ** -0.5)
    m1 = tab.reshape(steps, 2 * n1, n1 * nsub).astype(BF)
    rep = (jnp.arange(n1 * nsub, dtype=jnp.int32)[:, None] // nsub
           == jnp.arange(n1, dtype=jnp.int32)[None, :]).astype(BF)
    q = jnp.arange(n2, dtype=jnp.int32)
    ang3 = ((q[:, None] * q[None, :]) % n2).astype(F32) * (2.0 * math.pi / n2)
    c3, s3 = jnp.cos(ang3), jnp.sin(ang3)
    m3 = (jnp.concatenate([jnp.concatenate([c3, -s3], axis=1),
                           jnp.concatenate([s3, c3], axis=1)], axis=0) * (n2 ** -0.5)).astype(BF)
    return m1, rep, m3


def _fnet_group(x, g, wfold, *, row0, bsz, seq):
    t, d = x.shape
    n2 = FNET_N2
    n1 = seq // n2
    nsub = min(FNET_ROWS // n1, n2)
    kb = min(SUBLANES, n1)
    m1, rep, m3 = _fnet_tables(seq, nsub)
    rb = row0 // seq

    z = pl.pallas_call(
        _fnet1_kernel,
        grid=(bsz, n2 // nsub),
        in_specs=[
            pl.BlockSpec((n1, nsub, d), lambda b, j: (rb + b, j, 0)),
            _resident((1, d), lambda b, j: (0, 0)),
            pl.BlockSpec((1, 2 * n1, n1 * nsub), lambda b, j: (j, 0, 0)),
            _resident((n1 * nsub, n1), lambda b, j: (0, 0)),
        ],
        out_specs=pl.BlockSpec((None, 2, n1, nsub, d), lambda b, j: (b, 0, 0, j, 0)),
        out_shape=jax.ShapeDtypeStruct((bsz, 2, n1, n2, d), F32),
        compiler_params=_cparams(2, 56),
        name="fnet_stage1",
    )(x.reshape(t // n2, n2, d), g.reshape(1, d), m1, rep)

    out = pl.pallas_call(
        _fnet2_kernel,
        grid=(bsz, n1 // kb),
        in_specs=[
            pl.BlockSpec((None, 2, kb, n2, d), lambda b, j: (b, 0, j, 0, 0)),
            pl.BlockSpec((n2, kb, d), lambda b, j: (rb + b, j, 0)),
            _resident((2 * n2, 2 * n2), lambda b, j: (0, 0)),
            _resident((2 * d, d), lambda b, j: (0, 0)),
        ],
        out_specs=pl.BlockSpec((n2, kb, d), lambda b, j: (rb + b, j, 0)),
        out_shape=jax.ShapeDtypeStruct((t // n1, n1, d), F32),
        scratch_shapes=[pltpu.VMEM((kb * n2, 2 * d), BF)],
        input_output_aliases={1: 0},
        compiler_params=_cparams(2, 56),
        name="fnet_stage2",
    )(z, x.reshape(t // n1, n1, d), m3, wfold)
    return out.reshape(t, d)


def _sgu_kernel(x_ref, g_ref, win_ref, vg_ref, ws_ref, bs_ref, wout_ref, o_ref, v_ref, sv_ref, u_ref,
                *, tf):
    tm = x_ref.shape[0]
    half = vg_ref.shape[1]
    hd = half // GMLP_HEADS
    nc = half // tf
    x = x_ref[...]
    h = _rms(x, g_ref[...]).astype(BF)
    sqrt_half = math.sqrt(0.5)

    def gelu(a):
        return 0.5 * a * (1.0 + lax.erf(a * sqrt_half))

    ssq = jnp.zeros((tm, 1), F32)
    for c in range(nc):
        vc = gelu(_dot(h, win_ref[:, half + c * tf:half + (c + 1) * tf]))
        sv_ref[:, c * tf:(c + 1) * tf] = vc
        ssq = ssq + jnp.sum(vc * vc, axis=-1, keepdims=True)
    inv = lax.rsqrt(ssq * (1.0 / half) + EPS)
    for c in range(nc):
        cols = slice(c * tf, (c + 1) * tf)
        u_ref[:, cols] = gelu(_dot(h, win_ref[:, cols])).astype(BF)
        v_ref[:, cols] = (sv_ref[:, cols] * inv * vg_ref[:, cols]).astype(BF)
    nr = tm // CHUNK
    for hh in range(GMLP_HEADS):
        vv = jnp.concatenate([v_ref[r * CHUNK:(r + 1) * CHUNK, hh * hd:(hh + 1) * hd]
                              for r in range(nr)], axis=1)
        blk = _dot(ws_ref[hh], vv) + bs_ref[hh][:, 0:1]
        for r in range(nr):
            sv_ref[r * CHUNK:(r + 1) * CHUNK, hh * hd:(hh + 1) * hd] = blk[:, r * hd:(r + 1) * hd]
    acc = x
    for c in range(nc):
        cols = slice(c * tf, (c + 1) * tf)
        gated = (u_ref[:, cols].astype(F32) * sv_ref[:, cols]).astype(BF)
        acc = acc + _dot(gated, wout_ref[cols, :])
    o_ref[...] = acc


def _sgu_mixer(x, g, w_in, v_gain, w_s, b_s, w_out, *, tm, tf):
    t, d = x.shape
    half = v_gain.shape[0]
    bs = jnp.broadcast_to(b_s[:, :, None], (GMLP_HEADS, CHUNK, LANES)).astype(F32)
    return pl.pallas_call(
        functools.partial(_sgu_kernel, tf=tf),
        grid=(t // tm,),
        in_specs=[
            pl.BlockSpec((tm, d), lambda i: (i, 0)),
            _resident((1, d), lambda i: (0, 0)),
            _resident((d, 2 * half), lambda i: (0, 0)),
            _resident((1, half), lambda i: (0, 0)),
            _resident((GMLP_HEADS, CHUNK, CHUNK), lambda i: (0, 0, 0)),
            _resident((GMLP_HEADS, CHUNK, LANES), lambda i: (0, 0, 0)),
            _resident((half, d), lambda i: (0, 0)),
        ],
        out_specs=pl.BlockSpec((tm, d), lambda i: (i, 0)),
        out_shape=jax.ShapeDtypeStruct((t, d), F32),
        scratch_shapes=[pltpu.VMEM((tm, half), BF), pltpu.VMEM((tm, half), F32),
                        pltpu.VMEM((tm, half), BF)],
        compiler_params=_cparams(1, 56),
        name="sgu_mixer",
    )(x, g.reshape(1, d), w_in.astype(BF), v_gain.reshape(1, half), w_s.astype(BF), bs,
      w_out.astype(BF))


def _final_norm_kernel(x_ref, g_ref, o_ref):
    o_ref[...] = _rms(x_ref[...], g_ref[...])


def _final_norm(x, g, *, tm):
    t, d = x.shape
    return pl.pallas_call(
        _final_norm_kernel,
        grid=(t // tm,),
        in_specs=[pl.BlockSpec((tm, d), lambda i: (i, 0)), _resident((1, d), lambda i: (0, 0))],
        out_specs=pl.BlockSpec((tm, d), lambda i: (i, 0)),
        out_shape=jax.ShapeDtypeStruct((t, d), F32),
        compiler_params=_cparams(1, 32),
        name="final_norm",
    )(x, g.reshape(1, d))


def _tile(n, want):
    while n % want:
        want //= 2
    return want


def kernel(x_prompt, x_sample, norm_mix, norm_ffn, norm_final, conv_w_in, conv_w, conv_w_out, fnet_w_out, sgu_w_in, sgu_v_gain, sgu_w_s, sgu_b_s, sgu_w_out, ffn_w_gate, ffn_w_up, ffn_w_down, moe_w_router, moe_w_gate, moe_w_up, moe_w_down):
    bp, sp, d = x_prompt.shape
    bs_, ss, _ = x_sample.shape
    t_prompt = bp * sp
    t = t_prompt + bs_ * ss
    depth = norm_mix.shape[0]
    f = ffn_w_gate.shape[-1]
    tf = next(c for c in range(512, 0, -LANES) if f % c == 0)
    tm = _tile(math.gcd(sp, ss), 512)
    half = sgu_v_gain.shape[-1]

    zero1 = jnp.zeros((t // tm,), jnp.int32)
    all_tiles = jnp.full((1,), t // tm, jnp.int32)
    to_2d = lambda w: w.reshape(-1, w.shape[-1])
    conv_riders = (moe_w_gate, moe_w_down, ffn_w_gate, ffn_w_up, ffn_w_down, sgu_w_in, sgu_w_out)

    x = None
    for i in range(depth):
        m, j = i % N_MIXERS, i // N_MIXERS
        if m == 0:
            src = (x, None) if i else (x_prompt.reshape(t_prompt, d), x_sample.reshape(t - t_prompt, d))
            x, cast_a = _conv_mixer(*src, norm_mix[i], conv_w_in[j], conv_w[j], conv_w_out[j], tm=tm,
                                    t_prompt=t_prompt, s_prompt=sp, s_sample=ss,
                                    cast=() if i else [to_2d(w) for w in conv_riders])
            if i == 0:
                moe_g, moe_d, ffn_g, ffn_u, ffn_d, sgu_w_in, sgu_w_out = (
                    c.reshape(w.shape) for c, w in zip(cast_a, conv_riders))
                ffn_w = [w[:, None] for w in (ffn_g, ffn_u, ffn_d)]
        elif m == 1:
            wfold = _fnet_fold(fnet_w_out[j])
            x = _fnet_group(x, norm_mix[i], wfold, row0=0, bsz=bp, seq=sp)
            x = _fnet_group(x, norm_mix[i], wfold, row0=t_prompt, bsz=bs_, seq=ss)
        else:
            x = _sgu_mixer(x, norm_mix[i], sgu_w_in[j], sgu_v_gain[j], sgu_w_s[j], sgu_b_s[j],
                           sgu_w_out[j], tm=tm, tf=_tile(half, 512))
        k = i // 2
        last = i == depth - 1
        if i % 2 == 0:
            x, cast_b = _swiglu(x, norm_ffn[i], *ffn_w, k, zero1, all_tiles, tm=tm, tf=tf, dense=True,
                                cast=() if i else (to_2d(moe_w_up),))
            if i == 0:
                moe_w = [moe_g, cast_b[0].reshape(moe_w_up.shape), moe_d]
            if last:
                x = _final_norm(x, norm_final, tm=tm)
                x = (x[:t_prompt], x[t_prompt:])
        else:
            x = _moe_layer(x, norm_ffn[i], moe_w_router[k], *moe_w, k, norm_final, tm=tm, tf=tf,
                           t_prompt=t_prompt if last else None)
    return (x[0].reshape(bp, sp, d), x[1].reshape(bs_, ss, d))
```
